```python
import math
import jax
import jax.numpy as jnp
from jax import lax
import numpy as np

D_MODEL = 1024
BATCH = 4
SEQ = 8192
DEPTH = 2

EPS = 1e-6
M_HEADS = 4
M_DH = D_MODEL // 8
M_W = M_HEADS * M_DH
M_CHUNK = 128
CONV_K = 4
B_HEADS = 8
B_DH = D_MODEL // 16
B_W = B_HEADS * B_DH
MOBA_BLOCK = 256
MOBA_TOPK = 3
MOBA_QCHUNK = 64
C_HEADS = 8
C_DH = D_MODEL // 16
C_VDH = 2 * C_DH
C_W = C_HEADS * C_VDH
C_QBLOCK = 128
PEER_HEADS = 8
PEER_NKEYS = 128
PEER_N = PEER_NKEYS * PEER_NKEYS
PEER_DK = 256
PEER_TOPK = 16
PEER_TOK_CHUNK = 128

IN_EVEN = 4 * M_W + 2 * M_HEADS + 3 * B_W
MIX_EVEN = M_W + B_W
IN_ODD = 4 * C_HEADS * C_DH + C_HEADS * C_VDH
N_EVEN = (DEPTH + 1) // 2
N_ODD = DEPTH // 2

kernel_name = 'hybrid_mlstm_moba_diffattn_peer_block'


def rms_norm(x, g):
    xf = x.astype(jnp.float32)
    y = xf * lax.rsqrt(jnp.mean(xf * xf, axis=-1, keepdims=True) + EPS)
    return (y * g.astype(jnp.float32)).astype(x.dtype)


def alibi_slopes(n_heads):
    return jnp.asarray([2.0 ** (-8.0 * (h + 1) / n_heads) for h in range(n_heads)], dtype=jnp.float32)


def causal_depthwise_conv(x, w):
    ch = x.shape[-1]
    return lax.conv_general_dilated(x, w.astype(x.dtype)[:, None, :], window_strides=(1,),
                                    padding=[(CONV_K - 1, 0)],
                                    dimension_numbers=('NWC', 'WIO', 'NWC'),
                                    feature_group_count=ch)


def to_heads(t, n_heads):
    bn, s, _ = t.shape
    return t.reshape(bn, s, n_heads, -1).transpose(0, 2, 1, 3)


def from_heads(t):
    bn, h, s, dh = t.shape
    return t.transpose(0, 2, 1, 3).reshape(bn, s, h * dh)


def mlstm_chunkwise(q, k, v, i_pre, f_pre):
    f32 = jnp.float32
    bn, nh, s, dh = q.shape
    L = M_CHUNK
    nc = s // L
    q = q.astype(f32).reshape(bn, nh, nc, L, dh)
    k = (k.astype(f32) * dh ** -0.5).reshape(bn, nh, nc, L, dh)
    v = v.astype(f32).reshape(bn, nh, nc, L, dh)
    ig = i_pre.astype(f32).reshape(bn, nh, nc, L)
    b = jnp.cumsum(jax.nn.log_sigmoid(f_pre.astype(f32)).reshape(bn, nh, nc, L), axis=-1)
    b_end = b[..., -1]
    w_end = b_end[..., None] - b + ig
    m_loc = jnp.max(w_end, axis=-1)
    e_end = jnp.exp(w_end - m_loc[..., None])
    c_loc = jnp.einsum('bhcl,bhcld,bhcle->bhcde', e_end, k, v)
    n_loc = jnp.einsum('bhcl,bhcld->bhcd', e_end, k)

    def step(carry, inp):
        c_st, n_st, m_st = carry
        be, cl, nl, ml = inp
        m_new = jnp.maximum(be + m_st, ml)
        a = jnp.exp(be + m_st - m_new)
        sc = jnp.exp(ml - m_new)
        c_new = a[..., None, None] * c_st + sc[..., None, None] * cl
        n_new = a[..., None] * n_st + sc[..., None] * nl
        return (c_new, n_new, m_new), (c_st, n_st, m_st)

    init = (jnp.zeros((bn, nh, dh, dh), f32), jnp.zeros((bn, nh, dh), f32), jnp.zeros((bn, nh), f32))
    xs = tuple(jnp.moveaxis(t, 2, 0) for t in (b_end, c_loc, n_loc, m_loc))
    _, (c_prev, n_prev, m_prev) = lax.scan(step, init, xs)
    c_prev = jnp.moveaxis(c_prev, 0, 2)
    n_prev = jnp.moveaxis(n_prev, 0, 2)
    m_prev = jnp.moveaxis(m_prev, 0, 2)
    causal = jnp.tril(jnp.ones((L, L), dtype=bool))
    log_d = jnp.where(causal, b[..., :, None] - b[..., None, :] + ig[..., None, :], -jnp.inf)
    a_inter = b + m_prev[..., None]
    m_t = jnp.maximum(a_inter, jnp.max(log_d, axis=-1))
    w_inter = jnp.exp(a_inter - m_t)
    s_intra = jnp.einsum('bhcld,bhcsd->bhcls', q, k) * jnp.exp(log_d - m_t[..., None])
    num = (w_inter[..., None] * jnp.einsum('bhcld,bhcde->bhcle', q, c_prev)
           + jnp.einsum('bhcls,bhcse->bhcle', s_intra, v))
    den = w_inter * jnp.einsum('bhcld,bhcd->bhcl', q, n_prev) + jnp.sum(s_intra, axis=-1)
    h = num / jnp.maximum(jnp.abs(den), jnp.exp(-m_t))[..., None]
    return h.reshape(bn, nh, s, dh)


def moba_attention(q, k, v, slopes):
    f32 = jnp.float32
    bn, nh, s, dh = q.shape
    bs = MOBA_BLOCK
    nb = -(-s // bs)
    pad = nb * bs - s
    scale = dh ** -0.5
    q = q.astype(f32)
    k_blk = jnp.pad(k.astype(f32), ((0, 0), (0, 0), (0, pad), (0, 0))).reshape(bn, nh, nb, bs, dh)
    v_blk = jnp.pad(v.astype(f32), ((0, 0), (0, 0), (0, pad), (0, 0))).reshape(bn, nh, nb, bs, dh)
    k_sel_n = min(MOBA_TOPK, nb - 1)
    if k_sel_n > 0:
        gate = jnp.einsum('bhsd,bhnd->bhsn', q, jnp.mean(k_blk, axis=3))
        fully_past = jnp.arange(nb)[None, :] < (jnp.arange(s) // bs)[:, None]
        gate = jnp.where(fully_past, gate, -jnp.inf)
        gate_val, sel = lax.top_k(gate, k_sel_n)
        sel_ok = jnp.isfinite(gate_val)
    b_idx = jnp.arange(bn)[:, None, None, None]
    h_idx = jnp.arange(nh)[None, :, None, None]
    qc_n = MOBA_QCHUNK

    def one_chunk(ci):
        t0 = ci * qc_n
        qc = lax.dynamic_slice_in_dim(q, t0, qc_n, axis=2)
        tq = t0 + jnp.arange(qc_n)
        j = t0 // bs
        k_own = lax.dynamic_index_in_dim(k_blk, j, axis=2, keepdims=False)
        v_own = lax.dynamic_index_in_dim(v_blk, j, axis=2, keepdims=False)
        dist_own = (tq[:, None] - (j * bs + jnp.arange(bs))[None, :]).astype(f32)
        logit_own = jnp.einsum('bhqd,bhkd->bhqk', qc, k_own) * scale - slopes[:, None, None] * dist_own
        logit_own = jnp.where(dist_own >= 0, logit_own, -jnp.inf)
        if k_sel_n == 0:
            p = jax.nn.softmax(logit_own, axis=-1)
            return jnp.einsum('bhqk,bhkd->bhqd', p, v_own)
        sc = lax.dynamic_slice_in_dim(sel, t0, qc_n, axis=2)
        ok = lax.dynamic_slice_in_dim(sel_ok, t0, qc_n, axis=2)
        k_g = k_blk[b_idx, h_idx, sc]
        v_g = v_blk[b_idx, h_idx, sc]
        dist_g = (tq[:, None, None] - (sc[..., None] * bs + jnp.arange(bs))).astype(f32)
        logit_g = jnp.einsum('bhqd,bhqnkd->bhqnk', qc, k_g) * scale - slopes[:, None, None, None] * dist_g
        logit_g = jnp.where(ok[..., None], logit_g, -jnp.inf)
        logits = jnp.concatenate([logit_g.reshape(bn, nh, qc_n, k_sel_n * bs), logit_own], axis=-1)
        p = jax.nn.softmax(logits, axis=-1)
        p_g = p[..., :k_sel_n * bs].reshape(bn, nh, qc_n, k_sel_n, bs)
        p_own = p[..., k_sel_n * bs:]
        return jnp.einsum('bhqnk,bhqnkd->bhqd', p_g, v_g) + jnp.einsum('bhqk,bhkd->bhqd', p_own, v_own)

    out = lax.map(one_chunk, jnp.arange(s // qc_n))
    return out.transpose(1, 2, 0, 3, 4).reshape(bn, nh, s, dh)


def diff_attention(q, k, v, lam, slopes):
    f32 = jnp.float32
    bn, nh, _, s, dh = q.shape
    scale = dh ** -0.5
    qf = q.astype(f32)
    kf = k.astype(f32)
    vf = v.astype(f32)
    kpos = jnp.arange(s)
    qb_n = C_QBLOCK

    def one_block(ci):
        t0 = ci * qb_n
        qb = lax.dynamic_slice_in_dim(qf, t0, qb_n, axis=3)
        dist = ((t0 + jnp.arange(qb_n))[:, None] - kpos[None, :]).astype(f32)
        logits = jnp.einsum('bhcqd,bhckd->bhcqk', qb, kf) * scale - slopes[:, None, None, None] * dist
        logits = jnp.where(dist >= 0, logits, -jnp.inf)
        p = jax.nn.softmax(logits, axis=-1)
        return jnp.einsum('bhqk,bhkd->bhqd', p[:, :, 0] - lam * p[:, :, 1], vf)

    out = lax.map(one_block, jnp.arange(s // qb_n))
    return out.transpose(1, 2, 0, 3, 4).reshape(bn, nh, s, v.shape[-1])


def mlstm_moba_mixer(h, w_in, conv_w, igate_b, fgate_b, mnorm_g, qn_g, kn_g, w_out):
    f32 = jnp.float32
    z = h @ w_in
    offs = np.cumsum([M_W, M_W, M_W, M_W, M_HEADS, M_HEADS, B_W, B_W]).tolist()
    mq, mk, mv, mo, mi, mf, bq, bk, bv = jnp.split(z, offs, axis=-1)
    qk = jax.nn.silu(causal_depthwise_conv(jnp.concatenate([mq, mk], axis=-1), conv_w))
    mq, mk = qk[..., :M_W], qk[..., M_W:]
    i_pre = (mi + igate_b).transpose(0, 2, 1)
    f_pre = (mf + fgate_b).transpose(0, 2, 1)
    hm = mlstm_chunkwise(to_heads(mq, M_HEADS), to_heads(mk, M_HEADS), to_heads(mv, M_HEADS), i_pre, f_pre)
    hm = rms_norm(hm, mnorm_g[:, None, :]) * jax.nn.sigmoid(to_heads(mo, M_HEADS).astype(f32))
    qb = rms_norm(to_heads(bq, B_HEADS), qn_g)
    kb = rms_norm(to_heads(bk, B_HEADS), kn_g)
    hb = moba_attention(qb, kb, to_heads(bv, B_HEADS), alibi_slopes(B_HEADS))
    y = jnp.concatenate([from_heads(hm), from_heads(hb)], axis=-1).astype(h.dtype)
    return y @ w_out


def diff_attention_mixer(h, w_in, qn_g, kn_g, lam_p, onorm_g, w_out, lam_init):
    f32 = jnp.float32
    bn, s, _ = h.shape
    z = h @ w_in
    qk_w = 2 * C_HEADS * C_DH
    q = z[..., :qk_w].reshape(bn, s, C_HEADS, 2, C_DH).transpose(0, 2, 3, 1, 4)
    k = z[..., qk_w:2 * qk_w].reshape(bn, s, C_HEADS, 2, C_DH).transpose(0, 2, 3, 1, 4)
    v = to_heads(z[..., 2 * qk_w:], C_HEADS)
    q = rms_norm(q, qn_g)
    k = rms_norm(k, kn_g)
    lp = lam_p.astype(f32)
    lam = jnp.exp(jnp.sum(lp[0] * lp[1])) - jnp.exp(jnp.sum(lp[2] * lp[3])) + lam_init
    o = diff_attention(q, k, v, lam, alibi_slopes(C_HEADS))
    o = rms_norm(o, onorm_g) * (1.0 - lam_init)
    return from_heads(o).astype(h.dtype) @ w_out


def peer_ffn(x, wq, keys, u, v):
    f32 = jnp.float32
    bn, s, d = x.shape
    t = bn * s
    xt = x.reshape(t, d)
    qry = (xt @ wq).astype(f32).reshape(t, PEER_HEADS, 2, PEER_DK // 2)
    sub = jnp.einsum('thpd,pnd->thpn', qry, keys.astype(f32))
    s_top, i_top = lax.top_k(sub, PEER_TOPK)
    cand_s = (s_top[:, :, 0, :, None] + s_top[:, :, 1, None, :]).reshape(t, PEER_HEADS, PEER_TOPK * PEER_TOPK)
    cand_i = (i_top[:, :, 0, :, None] * PEER_NKEYS + i_top[:, :, 1, None, :]).reshape(t, PEER_HEADS, PEER_TOPK * PEER_TOPK)
    score, pos = lax.top_k(cand_s, PEER_TOPK)
    idx = jnp.take_along_axis(cand_i, pos, axis=-1)
    gate = jax.nn.softmax(score, axis=-1)
    tc = PEER_TOK_CHUNK
    nt = t // tc

    def one_chunk(args):
        xb, ib, gb = args
        act = jax.nn.gelu(jnp.einsum('td,thkd->thk', xb, u[ib]).astype(f32), approximate=False) * gb
        return jnp.einsum('thk,thkd->td', act, v[ib].astype(f32))

    out = lax.map(one_chunk, (xt.reshape(nt, tc, d),
                              idx.reshape(nt, tc, PEER_HEADS, PEER_TOPK),
                              gate.reshape(nt, tc, PEER_HEADS, PEER_TOPK)))
    return out.reshape(bn, s, d).astype(x.dtype)


def setup_inputs(seed: int = 0) -> dict:
    key = jax.random.key(seed)
    ks = jax.random.split(key, 24)
    f32 = jnp.float32
    D = D_MODEL

    def nrm(k, shape, sc):
        return jax.random.normal(k, shape, f32) * sc

    return {
        'x': nrm(ks[0], (BATCH, SEQ, D), 1.0),
        'c': nrm(ks[1], (BATCH, D), 1.0),
        'ada_w': nrm(ks[2], (DEPTH, D, 6 * D), 0.5 * D ** -0.5),
        'ada_b': nrm(ks[3], (DEPTH, 6 * D), 0.01),
        'norm_mix_g': 1.0 + nrm(ks[4], (DEPTH, D), 0.02),
        'norm_ffn_g': 1.0 + nrm(ks[5], (DEPTH, D), 0.02),
        'ev_w_in': nrm(ks[6], (N_EVEN, D, IN_EVEN), D ** -0.5),
        'ev_conv_w': nrm(ks[7], (N_EVEN, CONV_K, 2 * M_W), CONV_K ** -0.5),
        'ev_igate_b': -1.0 + nrm(ks[8], (N_EVEN, M_HEADS), 0.1),
        'ev_fgate_b': jnp.linspace(3.0, 6.0, M_HEADS, dtype=f32) + nrm(ks[9], (N_EVEN, M_HEADS), 0.1),
        'ev_mnorm_g': 1.0 + nrm(ks[10], (N_EVEN, M_HEADS, M_DH), 0.02),
        'ev_qn_g': 1.0 + nrm(ks[11], (N_EVEN, B_DH), 0.02),
        'ev_kn_g': 1.0 + nrm(ks[12], (N_EVEN, B_DH), 0.02),
        'ev_w_out': nrm(ks[13], (N_EVEN, MIX_EVEN, D), MIX_EVEN ** -0.5),
        'od_w_in': nrm(ks[14], (N_ODD, D, IN_ODD), D ** -0.5),
        'od_qn_g': 1.0 + nrm(ks[15], (N_ODD, C_DH), 0.02),
        'od_kn_g': 1.0 + nrm(ks[16], (N_ODD, C_DH), 0.02),
        'od_lam': nrm(ks[17], (N_ODD, 4, C_DH), 0.1),
        'od_onorm_g': 1.0 + nrm(ks[18], (N_ODD, C_VDH), 0.02),
        'od_w_out': nrm(ks[19], (N_ODD, C_W, D), C_W ** -0.5),
        'peer_wq': nrm(ks[20], (DEPTH, D, PEER_HEADS * PEER_DK), D ** -0.5),
        'peer_keys': nrm(ks[21], (DEPTH, 2, PEER_NKEYS, PEER_DK // 2), (PEER_DK // 2) ** -0.5),
        'peer_u': nrm(ks[22], (DEPTH, PEER_N, D), D ** -0.5),
        'peer_v': nrm(ks[23], (DEPTH, PEER_N, D), PEER_HEADS ** -0.5),
    }


def reference(x, c, ada_w, ada_b, norm_mix_g, norm_ffn_g, ev_w_in, ev_conv_w, ev_igate_b, ev_fgate_b,
              ev_mnorm_g, ev_qn_g, ev_kn_g, ev_w_out, od_w_in, od_qn_g, od_kn_g, od_lam, od_onorm_g,
              od_w_out, peer_wq, peer_keys, peer_u, peer_v):
    c_act = jax.nn.silu(c.astype(jnp.float32))
    for layer in range(DEPTH):
        mod = (c_act @ ada_w[layer].astype(jnp.float32) + ada_b[layer].astype(jnp.float32)).astype(x.dtype)
        sh1, sc1, g1, sh2, sc2, g2 = [m[:, None, :] for m in jnp.split(mod, 6, axis=-1)]
        h = rms_norm(x, norm_mix_g[layer]) * (1.0 + sc1) + sh1
        if layer % 2 == 0:
            e = layer // 2
            y = mlstm_moba_mixer(h, ev_w_in[e], ev_conv_w[e], ev_igate_b[e], ev_fgate_b[e],
                                 ev_mnorm_g[e], ev_qn_g[e], ev_kn_g[e], ev_w_out[e])
        else:
            o = layer // 2
            lam_init = 0.8 - 0.6 * math.exp(-0.3 * layer)
            y = diff_attention_mixer(h, od_w_in[o], od_qn_g[o], od_kn_g[o], od_lam[o],
                                     od_onorm_g[o], od_w_out[o], lam_init)
        x = x + g1 * y.astype(x.dtype)
        h = rms_norm(x, norm_ffn_g[layer]) * (1.0 + sc2) + sh2
        x = x + g2 * peer_ffn(h, peer_wq[layer], peer_keys[layer], peer_u[layer], peer_v[layer])
    return x
```

```python
import functools
import math

import jax
import jax.numpy as jnp
from jax import lax
from jax.experimental import pallas as pl
from jax.experimental.pallas import tpu as pltpu

F32 = jnp.float32
BF16 = jnp.bfloat16
HIGHEST = lax.Precision.HIGHEST
NEG_INF = float("-inf")

EPS = 1e-6
D_MODEL = 1024
M_HEADS, M_DH, M_CHUNK, CONV_K = 4, 128, 128, 4
M_W = M_HEADS * M_DH
B_HEADS, B_DH, MOBA_BLOCK, MOBA_TOPK = 8, 64, 256, 3
B_W = B_HEADS * B_DH
C_HEADS, C_DH, C_VDH = 8, 64, 128
PEER_HEADS, PEER_NKEYS, PEER_DK, PEER_TOPK = 8, 128, 256, 16

LANE = 128
VMEM_LIMIT = 56 * 1024 * 1024


def _cparams(sem, vmem=None):
    return pltpu.CompilerParams(dimension_semantics=sem, vmem_limit_bytes=vmem)


def _mod_kernel(c_ref, w_ref, b_ref, o_ref):
    c = c_ref[...]
    ca = c * (1.0 / (1.0 + jnp.exp(-c)))
    o_ref[...] = jnp.dot(ca, w_ref[...], precision=HIGHEST, preferred_element_type=F32) + b_ref[...]


def adaln_mod(c, ada_w, ada_b):
    depth, d, n = ada_w.shape
    bsz = c.shape[0]
    tn = 512
    return pl.pallas_call(
        _mod_kernel,
        grid=(depth, n // tn),
        in_specs=[pl.BlockSpec((bsz, d), lambda l, j: (0, 0)),
                  pl.BlockSpec((None, d, tn), lambda l, j: (l, 0, j)),
                  pl.BlockSpec((None, 1, tn), lambda l, j: (l, 0, j))],
        out_specs=pl.BlockSpec((None, bsz, tn), lambda l, j: (l, 0, j)),
        out_shape=jax.ShapeDtypeStruct((depth, bsz, n), F32),
        compiler_params=_cparams(("parallel", "parallel")),
        name="adaln_mod",
    )(c, ada_w, ada_b.reshape(depth, 1, n))


def _norm_linear_kernel(x_ref, g_ref, sc_ref, sh_ref, w_ref, b_ref, o_ref, *rest, emit_h, precise):
    if emit_h:
        h_out_ref, h_sc = rest
    else:
        (h_sc,) = rest

    @pl.when(pl.program_id(1) == 0)
    def _():
        x = x_ref[...]
        y = x * lax.rsqrt(jnp.mean(x * x, axis=-1, keepdims=True) + EPS) * g_ref[...]
        h = y * (1.0 + sc_ref[...]) + sh_ref[...]
        h_sc[...] = h.astype(h_sc.dtype)
        if emit_h:
            h_out_ref[...] = h

    if precise:
        acc = jnp.dot(h_sc[...], w_ref[...], precision=HIGHEST, preferred_element_type=F32)
    else:
        acc = jnp.dot(h_sc[...], w_ref[...], preferred_element_type=F32)
    o_ref[...] = (acc + b_ref[...]).astype(o_ref.dtype)


def norm_linear(x, g, sc, sh, w, bias, seq, *, tm=512, tn=512, emit_h=False, out_dtype=F32):
    t, d = x.shape
    n = w.shape[1]
    tn = min(tn, n)
    precise = w.dtype == F32
    rows_per_batch = seq // tm
    out_shape = [jax.ShapeDtypeStruct((t, n), out_dtype)]
    out_specs = [pl.BlockSpec((tm, tn), lambda i, j: (i, j))]
    if emit_h:
        out_shape.append(jax.ShapeDtypeStruct((t, d), F32))
        out_specs.append(pl.BlockSpec((tm, d), lambda i, j: (i, 0)))
    res = pl.pallas_call(
        functools.partial(_norm_linear_kernel, emit_h=emit_h, precise=precise),
        grid=(t // tm, n // tn),
        in_specs=[pl.BlockSpec((tm, d), lambda i, j: (i, 0)),
                  pl.BlockSpec((1, d), lambda i, j: (0, 0)),
                  pl.BlockSpec((None, 1, d), lambda i, j: (i // rows_per_batch, 0, 0)),
                  pl.BlockSpec((None, 1, d), lambda i, j: (i // rows_per_batch, 0, 0)),
                  pl.BlockSpec((d, tn), lambda i, j: (0, j)),
                  pl.BlockSpec((1, tn), lambda i, j: (0, j))],
        out_specs=out_specs,
        out_shape=out_shape,
        scratch_shapes=[pltpu.VMEM((tm, d), w.dtype)],
        compiler_params=_cparams(("parallel", "arbitrary"), VMEM_LIMIT),
        name="norm_linear",
    )(x, g.reshape(1, d), sc, sh, w, bias.reshape(1, n))
    return res if emit_h else res[0]


def _linear_residual_kernel(*refs, nparts):
    y_refs, w_refs = refs[:nparts], refs[nparts:2 * nparts]
    x_ref, gate_ref, o_ref = refs[2 * nparts:]
    acc = None
    for y_ref, w_ref in zip(y_refs, w_refs):
        part = jnp.dot(y_ref[...].astype(BF16), w_ref[...], preferred_element_type=F32)
        acc = part if acc is None else acc + part
    o_ref[...] = x_ref[...] + gate_ref[...] * acc


def linear_residual(ys, ws, x, gate, seq, *, tm=512):
    t, d = x.shape
    rows_per_batch = seq // tm
    return pl.pallas_call(
        functools.partial(_linear_residual_kernel, nparts=len(ys)),
        grid=(t // tm,),
        in_specs=([pl.BlockSpec((tm, y.shape[1]), lambda i: (i, 0)) for y in ys]
                  + [pl.BlockSpec(w.shape, lambda i: (0, 0)) for w in ws]
                  + [pl.BlockSpec((tm, d), lambda i: (i, 0)),
                     pl.BlockSpec((None, 1, d), lambda i: (i // rows_per_batch, 0, 0))]),
        out_specs=pl.BlockSpec((tm, d), lambda i: (i, 0)),
        out_shape=jax.ShapeDtypeStruct((t, d), F32),
        compiler_params=_cparams(("parallel",), VMEM_LIMIT),
        name="linear_residual",
    )(*ys, *ws, x, gate)


def _shift_rows(x, tail, s):
    if s == 0:
        return x
    xs = pltpu.roll(x, s, axis=0)
    ts = pltpu.roll(tail, s, axis=0)
    row = lax.broadcasted_iota(jnp.int32, (8, x.shape[1]), 0)
    top = jnp.where(row < s, ts, xs[0:8])
    return jnp.concatenate([top, xs[8:]], axis=0)


def _conv_silu(x, tail, w):
    acc = None
    for j in range(CONV_K):
        term = w[j:j + 1, :] * _shift_rows(x, tail, CONV_K - 1 - j)
        acc = term if acc is None else acc + term
    return acc * (1.0 / (1.0 + jnp.exp(-acc)))


def _log_sigmoid(x):
    return jnp.minimum(x, 0.0) - jnp.log(1.0 + jnp.exp(-jnp.abs(x)))


def _mlstm_kernel(q_ref, k_ref, v_ref, o_ref, wq_ref, wk_ref, icol_ref, fcol_ref, irow_ref, frow_ref,
                  g_ref, out_ref, c_sc, n_sc, m_sc, qt_sc, kt_sc):
    L = M_CHUNK

    @pl.when(pl.program_id(2) == 0)
    def _():
        c_sc[...] = jnp.zeros_like(c_sc)
        n_sc[...] = jnp.zeros_like(n_sc)
        m_sc[...] = jnp.zeros_like(m_sc)
        qt_sc[...] = jnp.zeros_like(qt_sc)
        kt_sc[...] = jnp.zeros_like(kt_sc)

    q_raw = q_ref[...]
    k_raw = k_ref[...]
    q = _conv_silu(q_raw, qt_sc[...], wq_ref[...])
    k = _conv_silu(k_raw, kt_sc[...], wk_ref[...]) * (M_DH ** -0.5)
    qt_sc[...] = q_raw[L - 8:, :]
    kt_sc[...] = k_raw[L - 8:, :]
    v = v_ref[...]

    row = lax.broadcasted_iota(jnp.int32, (L, L), 0)
    col = lax.broadcasted_iota(jnp.int32, (L, L), 1)
    causal = row >= col
    tri = jnp.where(causal, 1.0, 0.0).astype(F32)
    triu = jnp.where(row <= col, 1.0, 0.0).astype(F32)

    lf_col = _log_sigmoid(fcol_ref[...])
    lf_row = _log_sigmoid(frow_ref[...])
    ig_col = icol_ref[...]
    ig_row = irow_ref[...]
    b_col = jnp.dot(tri, jnp.broadcast_to(lf_col, (L, L)), precision=HIGHEST,
                    preferred_element_type=F32)[:, 0:1]
    b_row = jnp.dot(jnp.broadcast_to(lf_row, (8, L)), triu, precision=HIGHEST,
                    preferred_element_type=F32)[0:1, :]
    b_end = b_row[:, L - 1:L]

    m_prev = m_sc[...]
    c_prev = c_sc[...]
    n_prev = n_sc[...]

    log_d = jnp.where(causal, b_col - b_row + ig_row, NEG_INF)
    a_inter = b_col + m_prev
    m_t = jnp.maximum(a_inter, jnp.max(log_d, axis=1, keepdims=True))
    w_inter = jnp.exp(a_inter - m_t)
    qb = q.astype(BF16)
    kb = k.astype(BF16)
    vb = v.astype(BF16)
    s = lax.dot_general(qb, kb, (((1,), (1,)), ((), ())), preferred_element_type=F32)
    s = s * jnp.exp(log_d - m_t)
    num = (w_inter * jnp.dot(qb, c_prev.astype(BF16), preferred_element_type=F32)
           + jnp.dot(s.astype(BF16), vb, preferred_element_type=F32))
    den = (w_inter * jnp.sum(q * n_prev, axis=1, keepdims=True) + jnp.sum(s, axis=1, keepdims=True))
    h = num / jnp.maximum(jnp.abs(den), jnp.exp(-m_t))

    w_end_row = b_end - b_row + ig_row
    m_loc = jnp.max(w_end_row, axis=1, keepdims=True)
    e_col = jnp.exp(b_end - b_col + ig_col - m_loc)
    ke = k * e_col
    c_loc = lax.dot_general(ke.astype(BF16), vb, (((0,), (0,)), ((), ())), preferred_element_type=F32)
    n_loc = jnp.sum(ke, axis=0, keepdims=True)
    m_new = jnp.maximum(b_end + m_prev, m_loc)
    a = jnp.exp(b_end + m_prev - m_new)
    sc = jnp.exp(m_loc - m_new)
    c_sc[...] = a * c_prev + sc * c_loc
    n_sc[...] = a * n_prev + sc * n_loc
    m_sc[...] = m_new

    hn = h * lax.rsqrt(jnp.mean(h * h, axis=1, keepdims=True) + EPS) * g_ref[...]
    og = o_ref[...]
    out_ref[...] = (hn * (1.0 / (1.0 + jnp.exp(-og)))).astype(out_ref.dtype)


def mlstm_mixer(z, conv_w, icol, fcol, irow, frow, mnorm_g, bsz, seq):
    t = bsz * seq
    L = M_CHUNK
    nc = seq // L
    hq = M_W // LANE

    def zspec(sec):
        return pl.BlockSpec((L, M_DH), lambda b, h, c, sec=sec: (b * nc + c, sec * hq + h))

    return pl.pallas_call(
        _mlstm_kernel,
        grid=(bsz, M_HEADS, nc),
        in_specs=[zspec(0), zspec(1), zspec(2), zspec(3),
                  pl.BlockSpec((CONV_K, M_DH), lambda b, h, c: (0, h)),
                  pl.BlockSpec((CONV_K, M_DH), lambda b, h, c: (0, hq + h)),
                  pl.BlockSpec((None, L, 1), lambda b, h, c: (h, b * nc + c, 0)),
                  pl.BlockSpec((None, L, 1), lambda b, h, c: (h, b * nc + c, 0)),
                  pl.BlockSpec((None, None, 1, L), lambda b, h, c: (h, b * nc + c, 0, 0)),
                  pl.BlockSpec((None, None, 1, L), lambda b, h, c: (h, b * nc + c, 0, 0)),
                  pl.BlockSpec((None, 1, M_DH), lambda b, h, c: (h, 0, 0))],
        out_specs=pl.BlockSpec((L, M_DH), lambda b, h, c: (b * nc + c, h)),
        out_shape=jax.ShapeDtypeStruct((t, M_W), F32),
        scratch_shapes=[pltpu.VMEM((M_DH, M_DH), F32), pltpu.VMEM((1, M_DH), F32), pltpu.VMEM((1, 1), F32),
                        pltpu.VMEM((8, M_DH), F32), pltpu.VMEM((8, M_DH), F32)],
        compiler_params=_cparams(("parallel", "parallel", "arbitrary")),
        name="mlstm",
    )(z, z, z, z, conv_w, conv_w, icol, fcol, irow, frow, mnorm_g.reshape(M_HEADS, 1, M_DH))


def split_gates(zg):
    t = zg.shape[0]
    gt = zg[:, :2 * M_HEADS].T
    cols = gt.reshape(2 * M_HEADS, t, 1)
    rows = gt.reshape(2 * M_HEADS, t // M_CHUNK, 1, M_CHUNK)
    return cols[:M_HEADS], cols[M_HEADS:], rows[:M_HEADS], rows[M_HEADS:]


ATT_BLOCK = 256
HALF = 64


def _half_rms_norm(x, g):
    lo = lax.broadcasted_iota(jnp.int32, x.shape, 1) < HALF
    x2 = x * x
    s_lo = jnp.sum(jnp.where(lo, x2, 0.0), axis=1, keepdims=True)
    s_hi = jnp.sum(jnp.where(lo, 0.0, x2), axis=1, keepdims=True)
    r = jnp.where(lo, lax.rsqrt(s_lo * (1.0 / HALF) + EPS), lax.rsqrt(s_hi * (1.0 / HALF) + EPS))
    return x * r * g


def _head_norm_kernel(q_ref, k_ref, gq_ref, gk_ref, qn_ref, kn_ref, *rest):
    qn_ref[...] = _half_rms_norm(q_ref[...], gq_ref[...])
    kn = _half_rms_norm(k_ref[...], gk_ref[...])
    kn_ref[...] = kn.astype(kn_ref.dtype)
    if rest:
        rest[0][...] = jnp.mean(kn, axis=0, keepdims=True)


def head_norm(z, q_blk0, k_blk0, nblk, gq, gk, with_kmean):
    t = z.shape[0]
    w = nblk * LANE
    tb = ATT_BLOCK
    gq2 = jnp.concatenate([gq, gq]).reshape(1, LANE)
    gk2 = jnp.concatenate([gk, gk]).reshape(1, LANE)
    out_shape = [jax.ShapeDtypeStruct((t, w), F32), jax.ShapeDtypeStruct((t, w), BF16)]
    out_specs = [pl.BlockSpec((tb, LANE), lambda i, j: (i, j)), pl.BlockSpec((tb, LANE), lambda i, j: (i, j))]
    if with_kmean:
        out_shape.append(jax.ShapeDtypeStruct((t // tb, 1, w), F32))
        out_specs.append(pl.BlockSpec((None, 1, LANE), lambda i, j: (i, 0, j)))
    return pl.pallas_call(
        _head_norm_kernel,
        grid=(t // tb, nblk),
        in_specs=[pl.BlockSpec((tb, LANE), lambda i, j: (i, q_blk0 + j)),
                  pl.BlockSpec((tb, LANE), lambda i, j: (i, k_blk0 + j)),
                  pl.BlockSpec((1, LANE), lambda i, j: (0, 0)),
                  pl.BlockSpec((1, LANE), lambda i, j: (0, 0))],
        out_specs=out_specs,
        out_shape=out_shape,
        compiler_params=_cparams(("parallel", "parallel")),
        name="head_norm",
    )(z, z, gq2, gk2)


def _attn_kernel(*refs, moba, lam_init):
    if moba:
        (slope_ref, q_ref, k_ref, v_ref, km_ref, out_ref, sel_sc, m_sc, l_sc, acc_sc) = refs
    else:
        (slope_ref, q_ref, k_ref, v_ref, lam_ref, g_ref, out_ref, m_sc, l_sc, acc_sc) = refs
    tb = ATT_BLOCK
    grp = pl.program_id(1)
    j = pl.program_id(2)
    scale = HALF ** -0.5

    q = q_ref[...]
    lane = lax.broadcasted_iota(jnp.int32, (tb, LANE), 1)
    lo = lane < HALF
    qs = q * scale
    q_half = [jnp.where(lo, qs, 0.0).astype(BF16), jnp.where(lo, 0.0, qs).astype(BF16)]
    rel = (lax.broadcasted_iota(jnp.int32, (tb, tb), 0) - lax.broadcasted_iota(jnp.int32, (tb, tb), 1))
    relf = rel.astype(F32)
    if moba:
        slopes = [slope_ref[2 * grp], slope_ref[2 * grp + 1]]
    else:
        slopes = [slope_ref[grp], slope_ref[grp]]

    if moba:
        km = km_ref[...]
        for hf in range(2):
            qm = jnp.where(lo, q, 0.0) if hf == 0 else jnp.where(lo, 0.0, q)
            gate = lax.dot_general(qm, km, (((1,), (1,)), ((), ())), precision=HIGHEST,
                                   preferred_element_type=F32)
            gate = jnp.where(lane < j, gate, NEG_INF)
            sel = jnp.zeros((tb, LANE), F32)
            for _ in range(MOBA_TOPK):
                mx = jnp.max(gate, axis=1, keepdims=True)
                idx = jnp.min(jnp.where(gate == mx, lane, LANE), axis=1, keepdims=True)
                hit = lane == idx
                sel = jnp.where(jnp.logical_and(hit, mx > NEG_INF), 1.0, sel)
                gate = jnp.where(hit, NEG_INF, gate)
            sel_sc[hf] = sel.astype(BF16)

    def block_update(n, first):
        start = pl.multiple_of(n * tb, tb)
        kb = k_ref[pl.ds(start, tb), :]
        vb = v_ref[pl.ds(start, tb), :].astype(BF16)
        off = ((j - n) * tb).astype(F32)
        if moba and not first:
            onehot = (lax.broadcasted_iota(jnp.int32, (LANE, LANE), 0) == n).astype(BF16)
        for hf in range(2):
            logits = lax.dot_general(q_half[hf], kb, (((1,), (1,)), ((), ())), preferred_element_type=F32)
            logits = logits - slopes[hf] * (relf + off)
            if first:
                logits = jnp.where(rel >= 0, logits, NEG_INF)
            elif moba:
                chosen = jnp.dot(sel_sc[hf], onehot, preferred_element_type=F32)
                logits = jnp.where(jnp.concatenate([chosen, chosen], axis=1) > 0.5, logits, NEG_INF)
            mx = jnp.max(logits, axis=1, keepdims=True)
            if first:
                m_new = mx
                p = jnp.exp(logits - m_new)
                l_sc[hf] = jnp.sum(p, axis=1, keepdims=True)
                acc_sc[hf] = jnp.dot(p.astype(BF16), vb, preferred_element_type=F32)
            else:
                m_old = m_sc[hf]
                m_new = jnp.maximum(m_old, mx)
                alpha = jnp.exp(m_old - m_new)
                p = jnp.exp(logits - m_new)
                l_sc[hf] = alpha * l_sc[hf] + jnp.sum(p, axis=1, keepdims=True)
                acc_sc[hf] = alpha * acc_sc[hf] + jnp.dot(p.astype(BF16), vb, preferred_element_type=F32)
            m_sc[hf] = m_new

    block_update(j, True)

    def body(n, carry):
        block_update(n, False)
        return carry

    lax.fori_loop(0, j, body, 0)

    o0 = acc_sc[0] / l_sc[0]
    o1 = acc_sc[1] / l_sc[1]
    if moba:
        out_ref[...] = jnp.where(lo, o0, o1).astype(out_ref.dtype)
    else:
        lp = lam_ref[...]
        s1 = jnp.sum(lp[0:1] * lp[1:2], axis=1, keepdims=True)
        s2 = jnp.sum(lp[2:3] * lp[3:4], axis=1, keepdims=True)
        lam = jnp.exp(s1) - jnp.exp(s2) + lam_init
        o = o0 - lam * o1
        o = o * lax.rsqrt(jnp.mean(o * o, axis=1, keepdims=True) + EPS) * g_ref[...]
        out_ref[...] = (o * (1.0 - lam_init)).astype(out_ref.dtype)


def _alibi_slopes(n_heads):
    return jnp.asarray([2.0 ** (-8.0 * (h + 1) / n_heads) for h in range(n_heads)], dtype=F32)


def moba_attention(qn, kn, kmean, z, v_blk0, bsz, seq):
    t = bsz * seq
    tb = ATT_BLOCK
    nb = seq // tb
    ngrp = B_W // LANE
    km = jnp.pad(kmean.reshape(bsz, nb, B_W), ((0, 0), (0, LANE - nb), (0, 0)))
    return pl.pallas_call(
        functools.partial(_attn_kernel, moba=True, lam_init=None),
        grid=(bsz, ngrp, nb),
        in_specs=[pl.BlockSpec(memory_space=pltpu.SMEM),
                  pl.BlockSpec((tb, LANE), lambda b, g, j: (b * nb + j, g)),
                  pl.BlockSpec((seq, LANE), lambda b, g, j: (b, g)),
                  pl.BlockSpec((seq, LANE), lambda b, g, j: (b, v_blk0 + g)),
                  pl.BlockSpec((None, LANE, LANE), lambda b, g, j: (b, 0, g))],
        out_specs=pl.BlockSpec((tb, LANE), lambda b, g, j: (b * nb + j, g)),
        out_shape=jax.ShapeDtypeStruct((t, B_W), F32),
        scratch_shapes=[pltpu.VMEM((2, tb, LANE), BF16), pltpu.VMEM((2, tb, 1), F32),
                        pltpu.VMEM((2, tb, 1), F32), pltpu.VMEM((2, tb, LANE), F32)],
        compiler_params=_cparams(("parallel", "parallel", "arbitrary"), VMEM_LIMIT),
        name="moba_attention",
    )(_alibi_slopes(B_HEADS), qn, kn, z, km)


def diff_attention(qn, kn, z, v_blk0, lam_p, onorm_g, lam_init, bsz, seq):
    t = bsz * seq
    tb = ATT_BLOCK
    nb = seq // tb
    return pl.pallas_call(
        functools.partial(_attn_kernel, moba=False, lam_init=lam_init),
        grid=(bsz, C_HEADS, nb),
        in_specs=[pl.BlockSpec(memory_space=pltpu.SMEM),
                  pl.BlockSpec((tb, LANE), lambda b, g, j: (b * nb + j, g)),
                  pl.BlockSpec((seq, LANE), lambda b, g, j: (b, g)),
                  pl.BlockSpec((seq, LANE), lambda b, g, j: (b, v_blk0 + g)),
                  pl.BlockSpec((4, C_DH), lambda b, g, j: (0, 0)),
                  pl.BlockSpec((1, C_VDH), lambda b, g, j: (0, 0))],
        out_specs=pl.BlockSpec((tb, LANE), lambda b, g, j: (b * nb + j, g)),
        out_shape=jax.ShapeDtypeStruct((t, C_HEADS * C_VDH), F32),
        scratch_shapes=[pltpu.VMEM((2, tb, 1), F32), pltpu.VMEM((2, tb, 1), F32),
                        pltpu.VMEM((2, tb, LANE), F32)],
        compiler_params=_cparams(("parallel", "parallel", "arbitrary"), VMEM_LIMIT),
        name="diff_attention",
    )(_alibi_slopes(C_HEADS), qn, kn, z, lam_p, onorm_g.reshape(1, C_VDH))


def _top_rows(s, count):
    n = s.shape[0]
    rows = lax.broadcasted_iota(jnp.int32, s.shape, 0)
    vals, idxs = [], []
    for _ in range(count):
        mx = jnp.max(s, axis=0, keepdims=True)
        ix = jnp.min(jnp.where(s == mx, rows, n), axis=0, keepdims=True)
        vals.append(mx)
        idxs.append(ix)
        s = jnp.where(rows == ix, NEG_INF, s)
    return jnp.concatenate(vals, axis=0), jnp.concatenate(idxs, axis=0)


def _peer_route_kernel(q_ref, keys_ref, eid_ref, gate_ref):
    kk = PEER_TOPK
    half = PEER_DK // 2
    tops = []
    for p in range(2):
        qp = q_ref[:, p * half:(p + 1) * half]
        st = lax.dot_general(keys_ref[p], qp, (((1,), (1,)), ((), ())), precision=HIGHEST,
                             preferred_element_type=F32)
        tops.append(_top_rows(st, kk))
    (s0, i0), (s1, i1) = tops
    cand_s = jnp.concatenate([s0[a:a + 1] + s1 for a in range(kk)], axis=0)
    cand_i = jnp.concatenate([i0[a:a + 1] * PEER_NKEYS + i1 for a in range(kk)], axis=0)
    score, pos = _top_rows(cand_s, kk)
    rows = lax.broadcasted_iota(jnp.int32, cand_i.shape, 0)
    eid = jnp.concatenate(
        [jnp.sum(jnp.where(rows == pos[a:a + 1], cand_i, 0), axis=0, keepdims=True) for a in range(kk)], axis=0)
    e = jnp.exp(score - score[0:1])
    eid_ref[...] = eid
    gate_ref[...] = e / jnp.sum(e, axis=0, keepdims=True)


def peer_route(qry, keys, *, tt=256):
    t = qry.shape[0]
    hk = PEER_HEADS * PEER_TOPK
    return pl.pallas_call(
        _peer_route_kernel,
        grid=(t // tt, PEER_HEADS),
        in_specs=[pl.BlockSpec((tt, PEER_DK), lambda i, h: (i, h)),
                  pl.BlockSpec((2, PEER_NKEYS, PEER_DK // 2), lambda i, h: (0, 0, 0))],
        out_specs=[pl.BlockSpec((PEER_TOPK, tt), lambda i, h: (h, i)),
                   pl.BlockSpec((PEER_TOPK, tt), lambda i, h: (h, i))],
        out_shape=[jax.ShapeDtypeStruct((hk, t), jnp.int32), jax.ShapeDtypeStruct((hk, t), F32)],
        compiler_params=_cparams(("parallel", "parallel")),
        name="peer_route",
    )(qry, keys)


ROW_SUB = 4
PEER_TT = 64


def pack_table(tab):
    n, d = tab.shape
    b = lax.bitcast_convert_type(tab.astype(BF16), jnp.uint16).astype(jnp.uint32)
    w = b[:, :d // 2] | (b[:, d // 2:] << 16)
    return w.reshape(n, ROW_SUB, LANE)


def _unpack_row(w):
    lo = pltpu.bitcast(w << 16, F32)
    hi = pltpu.bitcast(w & jnp.uint32(0xFFFF0000), F32)
    return lo, hi


def _gelu(x):
    return 0.5 * x * (1.0 + lax.erf(x * (2.0 ** -0.5)))


def _peer_up_kernel(idx_ref, h_ref, gate_ref, tab_ref, act_ref, q_sc):
    hk = PEER_HEADS * PEER_TOPK
    ones = jnp.ones((8, LANE), F32)

    def token(t, carry):
        x = h_ref[t]
        x_lo = x[0:ROW_SUB]
        x_hi = x[ROW_SUB:]
        for k in range(hk):
            lo, hi = _unpack_row(tab_ref[idx_ref[t, k]])
            q_sc[pl.ds(k, 1), :] = jnp.sum(lo * x_lo + hi * x_hi, axis=0, keepdims=True)
        s = lax.dot_general(ones, q_sc[...], (((1,), (1,)), ((), ())), precision=HIGHEST,
                            preferred_element_type=F32)[0:1]
        act_ref[pl.ds(t, 1), :] = _gelu(s) * gate_ref[pl.ds(t, 1), :]
        return carry

    lax.fori_loop(0, PEER_TT, token, 0)


def _peer_down_kernel(idx_ref, act_ref, x_ref, g_ref, tab_ref, out_ref):
    hk = PEER_HEADS * PEER_TOPK
    g = g_ref[...]

    def token(t, carry):
        acc = [[jnp.zeros((ROW_SUB, LANE), F32) for _ in range(2)] for _ in range(2)]
        for k in range(hk):
            lo, hi = _unpack_row(tab_ref[idx_ref[t, k]])
            a = act_ref[t, k]
            acc[k % 2][0] = acc[k % 2][0] + a * lo
            acc[k % 2][1] = acc[k % 2][1] + a * hi
        y = jnp.concatenate([acc[0][0] + acc[1][0], acc[0][1] + acc[1][1]], axis=0)
        out_ref[t] = x_ref[t] + g * y
        return carry

    lax.fori_loop(0, PEER_TT, token, 0)


def _table_spec(n):
    return pl.BlockSpec((n, ROW_SUB, LANE), lambda i: (0, 0, 0), pipeline_mode=pl.Buffered(1))


def peer_up(idx, h, gate, tab_u):
    t, hk = idx.shape
    n = tab_u.shape[0]
    tt = PEER_TT
    return pl.pallas_call(
        _peer_up_kernel,
        grid=(t // tt,),
        in_specs=[pl.BlockSpec((tt, hk), lambda i: (i, 0), memory_space=pltpu.SMEM),
                  pl.BlockSpec((tt, 8, LANE), lambda i: (i, 0, 0)),
                  pl.BlockSpec((tt, hk), lambda i: (i, 0)),
                  _table_spec(n)],
        out_specs=pl.BlockSpec((tt, hk), lambda i: (i, 0)),
        out_shape=jax.ShapeDtypeStruct((t, hk), F32),
        scratch_shapes=[pltpu.VMEM((hk, LANE), F32)],
        compiler_params=_cparams(("arbitrary",), VMEM_LIMIT),
        name="peer_up",
    )(idx, h.reshape(t, 8, LANE), gate, tab_u)


def peer_down(idx, act, x, gate2, tab_v, seq):
    t, hk = idx.shape
    n = tab_v.shape[0]
    d = x.shape[1]
    tt = PEER_TT
    tiles_per_batch = seq // tt
    out = pl.pallas_call(
        _peer_down_kernel,
        grid=(t // tt,),
        in_specs=[pl.BlockSpec((tt, hk), lambda i: (i, 0), memory_space=pltpu.SMEM),
                  pl.BlockSpec((tt, hk), lambda i: (i, 0), memory_space=pltpu.SMEM),
                  pl.BlockSpec((tt, 8, LANE), lambda i: (i, 0, 0)),
                  pl.BlockSpec((None, 8, LANE), lambda i: (i // tiles_per_batch, 0, 0)),
                  _table_spec(n)],
        out_specs=pl.BlockSpec((tt, 8, LANE), lambda i: (i, 0, 0)),
        out_shape=jax.ShapeDtypeStruct((t, 8, LANE), F32),
        compiler_params=_cparams(("arbitrary",), VMEM_LIMIT),
        name="peer_down",
    )(idx, act, x.reshape(t, 8, LANE), gate2.reshape(-1, 8, LANE), tab_v)
    return out.reshape(t, d)


def peer_ffn(x, g, sc, sh, gate2, wq, keys, tab_u, tab_v, seq):
    qry, h = norm_linear(x, g, sc, sh, wq, jnp.zeros((wq.shape[1],), F32), seq, emit_h=True)
    eid_t, gate_t = peer_route(qry, keys)
    idx = eid_t.T
    act = peer_up(idx, h, gate_t.T, tab_u)
    return peer_down(idx, act, x, gate2, tab_v, seq)


def _even_mixer(xt, g, sc, sh, gate, w_in, conv_w, igate_b, fgate_b, mnorm_g, qn_g, kn_g, w_out, bsz, seq):
    d = xt.shape[1]
    n_gate = 2 * M_HEADS
    g0 = 4 * M_W
    w_main = jnp.concatenate([w_in[:, :g0], w_in[:, g0 + n_gate:]], axis=1).astype(BF16)
    w_gate = jnp.pad(w_in[:, g0:g0 + n_gate], ((0, 0), (0, LANE - n_gate)))
    b_gate = jnp.pad(jnp.concatenate([igate_b, fgate_b]), (0, LANE - n_gate))
    z = norm_linear(xt, g, sc, sh, w_main, jnp.zeros((w_main.shape[1],), F32), seq)
    zg = norm_linear(xt, g, sc, sh, w_gate, b_gate, seq)
    icol, fcol, irow, frow = split_gates(zg)
    hm = mlstm_mixer(z, conv_w, icol, fcol, irow, frow, mnorm_g, bsz, seq)
    blk = g0 // LANE
    nblk = B_W // LANE
    qn, kn, kmean = head_norm(z, blk, blk + nblk, nblk, qn_g, kn_g, True)
    hb = moba_attention(qn, kn, kmean, z, blk + 2 * nblk, bsz, seq)
    w_out = w_out.astype(BF16)
    return linear_residual([hm, hb], [w_out[:M_W], w_out[M_W:]], xt, gate, seq)


def _odd_mixer(xt, g, sc, sh, gate, w_in, qn_g, kn_g, lam_p, onorm_g, w_out, lam_init, bsz, seq):
    z = norm_linear(xt, g, sc, sh, w_in.astype(BF16), jnp.zeros((w_in.shape[1],), F32), seq)
    nblk = 2 * C_HEADS * C_DH // LANE
    qn, kn = head_norm(z, 0, nblk, nblk, qn_g, kn_g, False)
    o = diff_attention(qn, kn, z, 2 * nblk, lam_p, onorm_g, lam_init, bsz, seq)
    return linear_residual([o], [w_out.astype(BF16)], xt, gate, seq)


def kernel(x, c, ada_w, ada_b, norm_mix_g, norm_ffn_g, ev_w_in, ev_conv_w, ev_igate_b, ev_fgate_b, ev_mnorm_g,
           ev_qn_g, ev_kn_g, ev_w_out, od_w_in, od_qn_g, od_kn_g, od_lam, od_onorm_g, od_w_out, peer_wq,
           peer_keys, peer_u, peer_v):
    bsz, seq, d = x.shape
    depth = ada_w.shape[0]
    mod = adaln_mod(c, ada_w, ada_b).reshape(depth, bsz, 6, 1, d)
    xt = x.reshape(bsz * seq, d)
    for layer in range(depth):
        sh1, sc1, g1, sh2, sc2, g2 = [mod[layer, :, i] for i in range(6)]
        if layer % 2 == 0:
            e = layer // 2
            xt = _even_mixer(xt, norm_mix_g[layer], sc1, sh1, g1, ev_w_in[e], ev_conv_w[e], ev_igate_b[e],
                             ev_fgate_b[e], ev_mnorm_g[e], ev_qn_g[e], ev_kn_g[e], ev_w_out[e], bsz, seq)
        else:
            o = layer // 2
            lam_init = 0.8 - 0.6 * math.exp(-0.3 * layer)
            xt = _odd_mixer(xt, norm_mix_g[layer], sc1, sh1, g1, od_w_in[o], od_qn_g[o], od_kn_g[o], od_lam[o],
                            od_onorm_g[o], od_w_out[o], lam_init, bsz, seq)
        xt = peer_ffn(xt, norm_ffn_g[layer], sc2, sh2, g2, peer_wq[layer].astype(BF16), peer_keys[layer],
                      pack_table(peer_u[layer]), pack_table(peer_v[layer]), seq)
    return xt.reshape(bsz, seq, d)
```

```python
import functools
import math

import jax
import jax.numpy as jnp
from jax import lax
from jax.experimental import pallas as pl
from jax.experimental.pallas import tpu as pltpu

F32 = jnp.float32
BF16 = jnp.bfloat16
HIGHEST = lax.Precision.HIGHEST
NEG_INF = float("-inf")

EPS = 1e-6
D_MODEL = 1024
M_HEADS, M_DH, M_CHUNK, CONV_K = 4, 128, 128, 4
M_W = M_HEADS * M_DH
B_HEADS, B_DH, MOBA_BLOCK, MOBA_TOPK = 8, 64, 256, 3
B_W = B_HEADS * B_DH
C_HEADS, C_DH, C_VDH = 8, 64, 128
PEER_HEADS, PEER_NKEYS, PEER_DK, PEER_TOPK = 8, 128, 256, 16

LANE = 128
VMEM_LIMIT = 56 * 1024 * 1024


def _cparams(sem, vmem=None):
    return pltpu.CompilerParams(dimension_semantics=sem, vmem_limit_bytes=vmem)


def _mod_kernel(c_ref, w_ref, b_ref, o_ref):
    c = c_ref[...]
    ca = c * (1.0 / (1.0 + jnp.exp(-c)))
    o_ref[...] = jnp.dot(ca, w_ref[...], precision=HIGHEST, preferred_element_type=F32) + b_ref[...]


def adaln_mod(c, ada_w, ada_b):
    depth, d, n = ada_w.shape
    bsz = c.shape[0]
    tn = 512
    return pl.pallas_call(
        _mod_kernel,
        grid=(depth, n // tn),
        in_specs=[pl.BlockSpec((bsz, d), lambda l, j: (0, 0)),
                  pl.BlockSpec((None, d, tn), lambda l, j: (l, 0, j)),
                  pl.BlockSpec((None, 1, tn), lambda l, j: (l, 0, j))],
        out_specs=pl.BlockSpec((None, bsz, tn), lambda l, j: (l, 0, j)),
        out_shape=jax.ShapeDtypeStruct((depth, bsz, n), F32),
        compiler_params=_cparams(("parallel", "parallel")),
        name="adaln_mod",
    )(c, ada_w, ada_b.reshape(depth, 1, n))


def _norm_linear_kernel(x_ref, g_ref, sc_ref, sh_ref, w_ref, b_ref, o_ref, *rest, emit_h, precise):
    if emit_h:
        h_out_ref, h_sc = rest
    else:
        (h_sc,) = rest

    @pl.when(pl.program_id(1) == 0)
    def _():
        x = x_ref[...]
        y = x * lax.rsqrt(jnp.mean(x * x, axis=-1, keepdims=True) + EPS) * g_ref[...]
        h = y * (1.0 + sc_ref[...]) + sh_ref[...]
        h_sc[...] = h.astype(h_sc.dtype)
        if emit_h:
            h_out_ref[...] = h

    if precise:
        acc = jnp.dot(h_sc[...], w_ref[...], precision=HIGHEST, preferred_element_type=F32)
    else:
        acc = jnp.dot(h_sc[...], w_ref[...], preferred_element_type=F32)
    o_ref[...] = (acc + b_ref[...]).astype(o_ref.dtype)


def norm_linear(x, g, sc, sh, w, bias, seq, *, tm=512, tn=512, emit_h=False, out_dtype=F32):
    t, d = x.shape
    n = w.shape[1]
    tn = min(tn, n)
    precise = w.dtype == F32
    rows_per_batch = seq // tm
    out_shape = [jax.ShapeDtypeStruct((t, n), out_dtype)]
    out_specs = [pl.BlockSpec((tm, tn), lambda i, j: (i, j))]
    if emit_h:
        out_shape.append(jax.ShapeDtypeStruct((t, d), F32))
        out_specs.append(pl.BlockSpec((tm, d), lambda i, j: (i, 0)))
    res = pl.pallas_call(
        functools.partial(_norm_linear_kernel, emit_h=emit_h, precise=precise),
        grid=(t // tm, n // tn),
        in_specs=[pl.BlockSpec((tm, d), lambda i, j: (i, 0)),
                  pl.BlockSpec((1, d), lambda i, j: (0, 0)),
                  pl.BlockSpec((None, 1, d), lambda i, j: (i // rows_per_batch, 0, 0)),
                  pl.BlockSpec((None, 1, d), lambda i, j: (i // rows_per_batch, 0, 0)),
                  pl.BlockSpec((d, tn), lambda i, j: (0, j)),
                  pl.BlockSpec((1, tn), lambda i, j: (0, j))],
        out_specs=out_specs,
        out_shape=out_shape,
        scratch_shapes=[pltpu.VMEM((tm, d), w.dtype)],
        compiler_params=_cparams(("parallel", "arbitrary"), VMEM_LIMIT),
        name="norm_linear",
    )(x, g.reshape(1, d), sc, sh, w, bias.reshape(1, n))
    return res if emit_h else res[0]


def _linear_residual_kernel(*refs, nparts):
    y_refs, w_refs = refs[:nparts], refs[nparts:2 * nparts]
    x_ref, gate_ref, o_ref = refs[2 * nparts:]
    acc = None
    for y_ref, w_ref in zip(y_refs, w_refs):
        part = jnp.dot(y_ref[...].astype(BF16), w_ref[...], preferred_element_type=F32)
        acc = part if acc is None else acc + part
    o_ref[...] = x_ref[...] + gate_ref[...] * acc


def linear_residual(ys, ws, x, gate, seq, *, tm=512):
    t, d = x.shape
    rows_per_batch = seq // tm
    return pl.pallas_call(
        functools.partial(_linear_residual_kernel, nparts=len(ys)),
        grid=(t // tm,),
        in_specs=([pl.BlockSpec((tm, y.shape[1]), lambda i: (i, 0)) for y in ys]
                  + [pl.BlockSpec(w.shape, lambda i: (0, 0)) for w in ws]
                  + [pl.BlockSpec((tm, d), lambda i: (i, 0)),
                     pl.BlockSpec((None, 1, d), lambda i: (i // rows_per_batch, 0, 0))]),
        out_specs=pl.BlockSpec((tm, d), lambda i: (i, 0)),
        out_shape=jax.ShapeDtypeStruct((t, d), F32),
        compiler_params=_cparams(("parallel",), VMEM_LIMIT),
        name="linear_residual",
    )(*ys, *ws, x, gate)


def _shift_rows(x, tail, s):
    if s == 0:
        return x
    xs = pltpu.roll(x, s, axis=0)
    ts = pltpu.roll(tail, s, axis=0)
    row = lax.broadcasted_iota(jnp.int32, (8, x.shape[1]), 0)
    top = jnp.where(row < s, ts, xs[0:8])
    return jnp.concatenate([top, xs[8:]], axis=0)


def _conv_silu(x, tail, w):
    acc = None
    for j in range(CONV_K):
        term = w[j:j + 1, :] * _shift_rows(x, tail, CONV_K - 1 - j)
        acc = term if acc is None else acc + term
    return acc * (1.0 / (1.0 + jnp.exp(-acc)))


def _log_sigmoid(x):
    return jnp.minimum(x, 0.0) - jnp.log(1.0 + jnp.exp(-jnp.abs(x)))


def _mlstm_kernel(q_ref, k_ref, v_ref, o_ref, wq_ref, wk_ref, icol_ref, fcol_ref, irow_ref, frow_ref,
                  g_ref, out_ref, c_sc, n_sc, m_sc, qt_sc, kt_sc):
    L = M_CHUNK

    @pl.when(pl.program_id(2) == 0)
    def _():
        c_sc[...] = jnp.zeros_like(c_sc)
        n_sc[...] = jnp.zeros_like(n_sc)
        m_sc[...] = jnp.zeros_like(m_sc)
        qt_sc[...] = jnp.zeros_like(qt_sc)
        kt_sc[...] = jnp.zeros_like(kt_sc)

    q_raw = q_ref[...]
    k_raw = k_ref[...]
    q = _conv_silu(q_raw, qt_sc[...], wq_ref[...])
    k = _conv_silu(k_raw, kt_sc[...], wk_ref[...]) * (M_DH ** -0.5)
    qt_sc[...] = q_raw[L - 8:, :]
    kt_sc[...] = k_raw[L - 8:, :]
    v = v_ref[...]

    row = lax.broadcasted_iota(jnp.int32, (L, L), 0)
    col = lax.broadcasted_iota(jnp.int32, (L, L), 1)
    causal = row >= col
    tri = jnp.where(causal, 1.0, 0.0).astype(F32)
    triu = jnp.where(row <= col, 1.0, 0.0).astype(F32)

    lf_col = _log_sigmoid(fcol_ref[...])
    lf_row = _log_sigmoid(frow_ref[...])
    ig_col = icol_ref[...]
    ig_row = irow_ref[...]
    b_col = jnp.dot(tri, jnp.broadcast_to(lf_col, (L, L)), precision=HIGHEST,
                    preferred_element_type=F32)[:, 0:1]
    b_row = jnp.dot(jnp.broadcast_to(lf_row, (8, L)), triu, precision=HIGHEST,
                    preferred_element_type=F32)[0:1, :]
    b_end = b_row[:, L - 1:L]

    m_prev = m_sc[...]
    c_prev = c_sc[...]
    n_prev = n_sc[...]

    log_d = jnp.where(causal, b_col - b_row + ig_row, NEG_INF)
    a_inter = b_col + m_prev
    m_t = jnp.maximum(a_inter, jnp.max(log_d, axis=1, keepdims=True))
    w_inter = jnp.exp(a_inter - m_t)
    qb = q.astype(BF16)
    kb = k.astype(BF16)
    vb = v.astype(BF16)
    s = lax.dot_general(qb, kb, (((1,), (1,)), ((), ())), preferred_element_type=F32)
    s = s * jnp.exp(log_d - m_t)
    num = (w_inter * jnp.dot(qb, c_prev.astype(BF16), preferred_element_type=F32)
           + jnp.dot(s.astype(BF16), vb, preferred_element_type=F32))
    den = (w_inter * jnp.sum(q * n_prev, axis=1, keepdims=True) + jnp.sum(s, axis=1, keepdims=True))
    h = num / jnp.maximum(jnp.abs(den), jnp.exp(-m_t))

    w_end_row = b_end - b_row + ig_row
    m_loc = jnp.max(w_end_row, axis=1, keepdims=True)
    e_col = jnp.exp(b_end - b_col + ig_col - m_loc)
    ke = k * e_col
    c_loc = lax.dot_general(ke.astype(BF16), vb, (((0,), (0,)), ((), ())), preferred_element_type=F32)
    n_loc = jnp.sum(ke, axis=0, keepdims=True)
    m_new = jnp.maximum(b_end + m_prev, m_loc)
    a = jnp.exp(b_end + m_prev - m_new)
    sc = jnp.exp(m_loc - m_new)
    c_sc[...] = a * c_prev + sc * c_loc
    n_sc[...] = a * n_prev + sc * n_loc
    m_sc[...] = m_new

    hn = h * lax.rsqrt(jnp.mean(h * h, axis=1, keepdims=True) + EPS) * g_ref[...]
    og = o_ref[...]
    out_ref[...] = (hn * (1.0 / (1.0 + jnp.exp(-og)))).astype(out_ref.dtype)


def mlstm_mixer(z, conv_w, icol, fcol, irow, frow, mnorm_g, bsz, seq):
    t = bsz * seq
    L = M_CHUNK
    nc = seq // L
    hq = M_W // LANE

    def zspec(sec):
        return pl.BlockSpec((L, M_DH), lambda b, h, c, sec=sec: (b * nc + c, sec * hq + h))

    return pl.pallas_call(
        _mlstm_kernel,
        grid=(bsz, M_HEADS, nc),
        in_specs=[zspec(0), zspec(1), zspec(2), zspec(3),
                  pl.BlockSpec((CONV_K, M_DH), lambda b, h, c: (0, h)),
                  pl.BlockSpec((CONV_K, M_DH), lambda b, h, c: (0, hq + h)),
                  pl.BlockSpec((None, L, 1), lambda b, h, c: (h, b * nc + c, 0)),
                  pl.BlockSpec((None, L, 1), lambda b, h, c: (h, b * nc + c, 0)),
                  pl.BlockSpec((None, None, 1, L), lambda b, h, c: (h, b * nc + c, 0, 0)),
                  pl.BlockSpec((None, None, 1, L), lambda b, h, c: (h, b * nc + c, 0, 0)),
                  pl.BlockSpec((None, 1, M_DH), lambda b, h, c: (h, 0, 0))],
        out_specs=pl.BlockSpec((L, M_DH), lambda b, h, c: (b * nc + c, h)),
        out_shape=jax.ShapeDtypeStruct((t, M_W), F32),
        scratch_shapes=[pltpu.VMEM((M_DH, M_DH), F32), pltpu.VMEM((1, M_DH), F32), pltpu.VMEM((1, 1), F32),
                        pltpu.VMEM((8, M_DH), F32), pltpu.VMEM((8, M_DH), F32)],
        compiler_params=_cparams(("parallel", "parallel", "arbitrary")),
        name="mlstm",
    )(z, z, z, z, conv_w, conv_w, icol, fcol, irow, frow, mnorm_g.reshape(M_HEADS, 1, M_DH))


def split_gates(zg):
    t = zg.shape[0]
    gt = zg[:, :2 * M_HEADS].T
    cols = gt.reshape(2 * M_HEADS, t, 1)
    rows = gt.reshape(2 * M_HEADS, t // M_CHUNK, 1, M_CHUNK)
    return cols[:M_HEADS], cols[M_HEADS:], rows[:M_HEADS], rows[M_HEADS:]


ATT_BLOCK = 256
HALF = 64


def _half_rms_norm(x, g):
    lo = lax.broadcasted_iota(jnp.int32, x.shape, 1) < HALF
    x2 = x * x
    s_lo = jnp.sum(jnp.where(lo, x2, 0.0), axis=1, keepdims=True)
    s_hi = jnp.sum(jnp.where(lo, 0.0, x2), axis=1, keepdims=True)
    r = jnp.where(lo, lax.rsqrt(s_lo * (1.0 / HALF) + EPS), lax.rsqrt(s_hi * (1.0 / HALF) + EPS))
    return x * r * g


def _head_norm_kernel(q_ref, k_ref, gq_ref, gk_ref, qn_ref, kn_ref, *rest):
    qn_ref[...] = _half_rms_norm(q_ref[...], gq_ref[...])
    kn = _half_rms_norm(k_ref[...], gk_ref[...])
    kn_ref[...] = kn.astype(kn_ref.dtype)
    if rest:
        rest[0][...] = jnp.mean(kn, axis=0, keepdims=True)


def head_norm(z, q_blk0, k_blk0, nblk, gq, gk, with_kmean):
    t = z.shape[0]
    w = nblk * LANE
    tb = ATT_BLOCK
    gq2 = jnp.concatenate([gq, gq]).reshape(1, LANE)
    gk2 = jnp.concatenate([gk, gk]).reshape(1, LANE)
    out_shape = [jax.ShapeDtypeStruct((t, w), F32), jax.ShapeDtypeStruct((t, w), BF16)]
    out_specs = [pl.BlockSpec((tb, LANE), lambda i, j: (i, j)), pl.BlockSpec((tb, LANE), lambda i, j: (i, j))]
    if with_kmean:
        out_shape.append(jax.ShapeDtypeStruct((t // tb, 1, w), F32))
        out_specs.append(pl.BlockSpec((None, 1, LANE), lambda i, j: (i, 0, j)))
    return pl.pallas_call(
        _head_norm_kernel,
        grid=(t // tb, nblk),
        in_specs=[pl.BlockSpec((tb, LANE), lambda i, j: (i, q_blk0 + j)),
                  pl.BlockSpec((tb, LANE), lambda i, j: (i, k_blk0 + j)),
                  pl.BlockSpec((1, LANE), lambda i, j: (0, 0)),
                  pl.BlockSpec((1, LANE), lambda i, j: (0, 0))],
        out_specs=out_specs,
        out_shape=out_shape,
        compiler_params=_cparams(("parallel", "parallel")),
        name="head_norm",
    )(z, z, gq2, gk2)


def _attn_kernel(*refs, moba, lam_init):
    if moba:
        (slope_ref, q_ref, k_ref, v_ref, km_ref, out_ref, sel_sc, m_sc, l_sc, acc_sc) = refs
    else:
        (slope_ref, q_ref, k_ref, v_ref, lam_ref, g_ref, out_ref, m_sc, l_sc, acc_sc) = refs
    tb = ATT_BLOCK
    grp = pl.program_id(1)
    j = pl.program_id(2)
    scale = HALF ** -0.5

    q = q_ref[...]
    lane = lax.broadcasted_iota(jnp.int32, (tb, LANE), 1)
    lo = lane < HALF
    qs = q * scale
    q_half = [jnp.where(lo, qs, 0.0).astype(BF16), jnp.where(lo, 0.0, qs).astype(BF16)]
    rel = (lax.broadcasted_iota(jnp.int32, (tb, tb), 0) - lax.broadcasted_iota(jnp.int32, (tb, tb), 1))
    relf = rel.astype(F32)
    if moba:
        slopes = [slope_ref[2 * grp], slope_ref[2 * grp + 1]]
    else:
        slopes = [slope_ref[grp], slope_ref[grp]]

    if moba:
        km = km_ref[...]
        for hf in range(2):
            qm = jnp.where(lo, q, 0.0) if hf == 0 else jnp.where(lo, 0.0, q)
            gate = lax.dot_general(qm, km, (((1,), (1,)), ((), ())), precision=HIGHEST,
                                   preferred_element_type=F32)
            gate = jnp.where(lane < j, gate, NEG_INF)
            sel = jnp.zeros((tb, LANE), F32)
            for _ in range(MOBA_TOPK):
                mx = jnp.max(gate, axis=1, keepdims=True)
                idx = jnp.min(jnp.where(gate == mx, lane, LANE), axis=1, keepdims=True)
                hit = lane == idx
                sel = jnp.where(jnp.logical_and(hit, mx > NEG_INF), 1.0, sel)
                gate = jnp.where(hit, NEG_INF, gate)
            sel_sc[hf] = sel.astype(BF16)

    def block_update(n, first):
        start = pl.multiple_of(n * tb, tb)
        kb = k_ref[pl.ds(start, tb), :]
        vb = v_ref[pl.ds(start, tb), :].astype(BF16)
        off = ((j - n) * tb).astype(F32)
        if moba and not first:
            onehot = (lax.broadcasted_iota(jnp.int32, (LANE, LANE), 0) == n).astype(BF16)
        for hf in range(2):
            logits = lax.dot_general(q_half[hf], kb, (((1,), (1,)), ((), ())), preferred_element_type=F32)
            logits = logits - slopes[hf] * (relf + off)
            if first:
                logits = jnp.where(rel >= 0, logits, NEG_INF)
            elif moba:
                chosen = jnp.dot(sel_sc[hf], onehot, preferred_element_type=F32)
                logits = jnp.where(jnp.concatenate([chosen, chosen], axis=1) > 0.5, logits, NEG_INF)
            mx = jnp.max(logits, axis=1, keepdims=True)
            if first:
                m_new = mx
                p = jnp.exp(logits - m_new)
                l_sc[hf] = jnp.sum(p, axis=1, keepdims=True)
                acc_sc[hf] = jnp.dot(p.astype(BF16), vb, preferred_element_type=F32)
            else:
                m_old = m_sc[hf]
                m_new = jnp.maximum(m_old, mx)
                alpha = jnp.exp(m_old - m_new)
                p = jnp.exp(logits - m_new)
                l_sc[hf] = alpha * l_sc[hf] + jnp.sum(p, axis=1, keepdims=True)
                acc_sc[hf] = alpha * acc_sc[hf] + jnp.dot(p.astype(BF16), vb, preferred_element_type=F32)
            m_sc[hf] = m_new

    block_update(j, True)

    def body(n, carry):
        block_update(n, False)
        return carry

    lax.fori_loop(0, j, body, 0)

    o0 = acc_sc[0] / l_sc[0]
    o1 = acc_sc[1] / l_sc[1]
    if moba:
        out_ref[...] = jnp.where(lo, o0, o1).astype(out_ref.dtype)
    else:
        lp = lam_ref[...]
        s1 = jnp.sum(lp[0:1] * lp[1:2], axis=1, keepdims=True)
        s2 = jnp.sum(lp[2:3] * lp[3:4], axis=1, keepdims=True)
        lam = jnp.exp(s1) - jnp.exp(s2) + lam_init
        o = o0 - lam * o1
        o = o * lax.rsqrt(jnp.mean(o * o, axis=1, keepdims=True) + EPS) * g_ref[...]
        out_ref[...] = (o * (1.0 - lam_init)).astype(out_ref.dtype)


def _alibi_slopes(n_heads):
    return jnp.asarray([2.0 ** (-8.0 * (h + 1) / n_heads) for h in range(n_heads)], dtype=F32)


def moba_attention(qn, kn, kmean, z, v_blk0, bsz, seq):
    t = bsz * seq
    tb = ATT_BLOCK
    nb = seq // tb
    ngrp = B_W // LANE
    km = jnp.pad(kmean.reshape(bsz, nb, B_W), ((0, 0), (0, LANE - nb), (0, 0)))
    return pl.pallas_call(
        functools.partial(_attn_kernel, moba=True, lam_init=None),
        grid=(bsz, ngrp, nb),
        in_specs=[pl.BlockSpec(memory_space=pltpu.SMEM),
                  pl.BlockSpec((tb, LANE), lambda b, g, j: (b * nb + j, g)),
                  pl.BlockSpec((seq, LANE), lambda b, g, j: (b, g)),
                  pl.BlockSpec((seq, LANE), lambda b, g, j: (b, v_blk0 + g)),
                  pl.BlockSpec((None, LANE, LANE), lambda b, g, j: (b, 0, g))],
        out_specs=pl.BlockSpec((tb, LANE), lambda b, g, j: (b * nb + j, g)),
        out_shape=jax.ShapeDtypeStruct((t, B_W), F32),
        scratch_shapes=[pltpu.VMEM((2, tb, LANE), BF16), pltpu.VMEM((2, tb, 1), F32),
                        pltpu.VMEM((2, tb, 1), F32), pltpu.VMEM((2, tb, LANE), F32)],
        compiler_params=_cparams(("parallel", "parallel", "arbitrary"), VMEM_LIMIT),
        name="moba_attention",
    )(_alibi_slopes(B_HEADS), qn, kn, z, km)


def diff_attention(qn, kn, z, v_blk0, lam_p, onorm_g, lam_init, bsz, seq):
    t = bsz * seq
    tb = ATT_BLOCK
    nb = seq // tb
    return pl.pallas_call(
        functools.partial(_attn_kernel, moba=False, lam_init=lam_init),
        grid=(bsz, C_HEADS, nb),
        in_specs=[pl.BlockSpec(memory_space=pltpu.SMEM),
                  pl.BlockSpec((tb, LANE), lambda b, g, j: (b * nb + j, g)),
                  pl.BlockSpec((seq, LANE), lambda b, g, j: (b, g)),
                  pl.BlockSpec((seq, LANE), lambda b, g, j: (b, v_blk0 + g)),
                  pl.BlockSpec((4, C_DH), lambda b, g, j: (0, 0)),
                  pl.BlockSpec((1, C_VDH), lambda b, g, j: (0, 0))],
        out_specs=pl.BlockSpec((tb, LANE), lambda b, g, j: (b * nb + j, g)),
        out_shape=jax.ShapeDtypeStruct((t, C_HEADS * C_VDH), F32),
        scratch_shapes=[pltpu.VMEM((2, tb, 1), F32), pltpu.VMEM((2, tb, 1), F32),
                        pltpu.VMEM((2, tb, LANE), F32)],
        compiler_params=_cparams(("parallel", "parallel", "arbitrary"), VMEM_LIMIT),
        name="diff_attention",
    )(_alibi_slopes(C_HEADS), qn, kn, z, lam_p, onorm_g.reshape(1, C_VDH))


def _top_rows(s, count):
    n = s.shape[0]
    rows = lax.broadcasted_iota(jnp.int32, s.shape, 0)
    vals, idxs = [], []
    for _ in range(count):
        mx = jnp.max(s, axis=0, keepdims=True)
        ix = jnp.min(jnp.where(s == mx, rows, n), axis=0, keepdims=True)
        vals.append(mx)
        idxs.append(ix)
        s = jnp.where(rows == ix, NEG_INF, s)
    return jnp.concatenate(vals, axis=0), jnp.concatenate(idxs, axis=0)


def _peer_route_kernel(q_ref, keys_ref, eid_ref, gate_ref):
    kk = PEER_TOPK
    half = PEER_DK // 2
    tops = []
    for p in range(2):
        qp = q_ref[:, p * half:(p + 1) * half]
        st = lax.dot_general(keys_ref[p], qp, (((1,), (1,)), ((), ())), precision=HIGHEST,
                             preferred_element_type=F32)
        tops.append(_top_rows(st, kk))
    (s0, i0), (s1, i1) = tops
    cand_s = jnp.concatenate([s0[a:a + 1] + s1 for a in range(kk)], axis=0)
    cand_i = jnp.concatenate([i0[a:a + 1] * PEER_NKEYS + i1 for a in range(kk)], axis=0)
    score, pos = _top_rows(cand_s, kk)
    rows = lax.broadcasted_iota(jnp.int32, cand_i.shape, 0)
    eid = jnp.concatenate(
        [jnp.sum(jnp.where(rows == pos[a:a + 1], cand_i, 0), axis=0, keepdims=True) for a in range(kk)], axis=0)
    e = jnp.exp(score - score[0:1])
    eid_ref[...] = eid
    gate_ref[...] = e / jnp.sum(e, axis=0, keepdims=True)


def peer_route(qry, keys, *, tt=256):
    t = qry.shape[0]
    hk = PEER_HEADS * PEER_TOPK
    return pl.pallas_call(
        _peer_route_kernel,
        grid=(t // tt, PEER_HEADS),
        in_specs=[pl.BlockSpec((tt, PEER_DK), lambda i, h: (i, h)),
                  pl.BlockSpec((2, PEER_NKEYS, PEER_DK // 2), lambda i, h: (0, 0, 0))],
        out_specs=[pl.BlockSpec((PEER_TOPK, tt), lambda i, h: (h, i)),
                   pl.BlockSpec((PEER_TOPK, tt), lambda i, h: (h, i))],
        out_shape=[jax.ShapeDtypeStruct((hk, t), jnp.int32), jax.ShapeDtypeStruct((hk, t), F32)],
        compiler_params=_cparams(("parallel", "parallel")),
        name="peer_route",
    )(qry, keys)


ROW_SUB = 4
PEER_TT = 64


PEER_HK = PEER_HEADS * PEER_TOPK
ROW_BF = 2 * ROW_SUB
GROUP = 8


def pack_table(tab):
    n, d = tab.shape
    b = lax.bitcast_convert_type(tab.astype(BF16), jnp.uint16).astype(jnp.uint32)
    b = b.reshape(n, ROW_SUB, 2, LANE)
    return b[:, :, 0, :] | (b[:, :, 1, :] << 16)


def _split_bf16(x):
    hi = x.astype(BF16)
    lo = (x - hi.astype(F32)).astype(BF16)
    return jnp.concatenate([hi, lo], axis=0)


def _gelu(x):
    return 0.5 * x * (1.0 + lax.erf(x * (2.0 ** -0.5)))


def _gather_rows(idx_ref, tab_ref, dst_ref, t):
    for k in range(PEER_HK):
        dst_ref[k * ROW_SUB:(k + 1) * ROW_SUB, :] = tab_ref[idx_ref[t, k]]


def _diag_mask(rows):
    shape = (rows, PEER_HK * ROW_BF)
    return (lax.broadcasted_iota(jnp.int32, shape, 1) % ROW_BF) == (lax.broadcasted_iota(jnp.int32, shape, 0) % ROW_BF)


def _peer_up_kernel(idx_ref, h_ref, gate_ref, tab_ref, act_ref, g0_sc, g1_sc, c_sc):
    slots = (g0_sc, g1_sc)
    wide = PEER_HK * ROW_BF
    mask = _diag_mask(2 * ROW_BF)
    fold = (lax.broadcasted_iota(jnp.int32, (wide, PEER_HK), 0) // ROW_BF
            == lax.broadcasted_iota(jnp.int32, (wide, PEER_HK), 1)).astype(BF16)

    def group(g, carry):
        t0 = g * GROUP
        _gather_rows(idx_ref, tab_ref, slots[0], t0)
        for i in range(GROUP):
            if i + 1 < GROUP:
                _gather_rows(idx_ref, tab_ref, slots[(i + 1) % 2], t0 + i + 1)
            rows = pltpu.bitcast(slots[i % 2][...], BF16)
            xs = _split_bf16(h_ref[t0 + i])
            y = lax.dot_general(xs, rows, (((1,), (1,)), ((), ())), preferred_element_type=F32)
            y = jnp.where(mask, y, 0.0)
            c_sc[i:i + 1, :] = jnp.sum(y, axis=0, keepdims=True)
        s = jnp.dot(_split_bf16(c_sc[...]), fold, preferred_element_type=F32)
        s = s[0:GROUP] + s[GROUP:]
        rows8 = pl.ds(pl.multiple_of(t0, GROUP), GROUP)
        act_ref[rows8, :] = _gelu(s) * gate_ref[rows8, :]
        return carry

    lax.fori_loop(0, PEER_TT // GROUP, group, 0)


def _peer_down_kernel(idx_ref, act_ref, x_ref, g_ref, tab_ref, out_ref, g0_sc, g1_sc):
    slots = (g0_sc, g1_sc)
    wide = PEER_HK * ROW_BF
    mask = _diag_mask(ROW_BF)
    spread = (lax.broadcasted_iota(jnp.int32, (PEER_HK, wide), 1) // ROW_BF
              == lax.broadcasted_iota(jnp.int32, (PEER_HK, wide), 0)).astype(BF16)
    gate = g_ref[...]

    def group(g, carry):
        t0 = g * GROUP
        _gather_rows(idx_ref, tab_ref, slots[0], t0)
        acts = act_ref[pl.ds(pl.multiple_of(t0, GROUP), GROUP), :]
        a_wide = jnp.dot(_split_bf16(acts), spread, preferred_element_type=F32)
        for i in range(GROUP):
            if i + 1 < GROUP:
                _gather_rows(idx_ref, tab_ref, slots[(i + 1) % 2], t0 + i + 1)
            rows = pltpu.bitcast(slots[i % 2][...], BF16)
            a_hi = jnp.where(mask, jnp.broadcast_to(a_wide[i:i + 1], (ROW_BF, wide)), 0.0)
            a_lo = jnp.where(mask, jnp.broadcast_to(a_wide[GROUP + i:GROUP + i + 1], (ROW_BF, wide)), 0.0)
            lhs = jnp.concatenate([a_hi, a_lo], axis=0).astype(BF16)
            y = jnp.dot(lhs, rows, preferred_element_type=F32)
            out_ref[t0 + i] = x_ref[t0 + i] + gate * (y[0:ROW_BF] + y[ROW_BF:])
        return carry

    lax.fori_loop(0, PEER_TT // GROUP, group, 0)


def _table_spec(n):
    return pl.BlockSpec((n, ROW_SUB, LANE), lambda i: (0, 0, 0), pipeline_mode=pl.Buffered(1))


def peer_up(idx, h, gate, tab_u):
    t, hk = idx.shape
    n = tab_u.shape[0]
    tt = PEER_TT
    return pl.pallas_call(
        _peer_up_kernel,
        grid=(t // tt,),
        in_specs=[pl.BlockSpec((tt, hk), lambda i: (i, 0), memory_space=pltpu.SMEM),
                  pl.BlockSpec((tt, 8, LANE), lambda i: (i, 0, 0)),
                  pl.BlockSpec((tt, hk), lambda i: (i, 0)),
                  _table_spec(n)],
        out_specs=pl.BlockSpec((tt, hk), lambda i: (i, 0)),
        out_shape=jax.ShapeDtypeStruct((t, hk), F32),
        scratch_shapes=[pltpu.VMEM((hk * ROW_SUB, LANE), jnp.uint32), pltpu.VMEM((hk * ROW_SUB, LANE), jnp.uint32),
                        pltpu.VMEM((GROUP, hk * ROW_BF), F32)],
        compiler_params=_cparams(("arbitrary",), VMEM_LIMIT),
        name="peer_up",
    )(idx, h.reshape(t, 8, LANE), gate, tab_u)


def peer_down(idx, act, x, gate2, tab_v, seq):
    t, hk = idx.shape
    n = tab_v.shape[0]
    d = x.shape[1]
    tt = PEER_TT
    tiles_per_batch = seq // tt
    out = pl.pallas_call(
        _peer_down_kernel,
        grid=(t // tt,),
        in_specs=[pl.BlockSpec((tt, hk), lambda i: (i, 0), memory_space=pltpu.SMEM),
                  pl.BlockSpec((tt, hk), lambda i: (i, 0)),
                  pl.BlockSpec((tt, 8, LANE), lambda i: (i, 0, 0)),
                  pl.BlockSpec((None, 8, LANE), lambda i: (i // tiles_per_batch, 0, 0)),
                  _table_spec(n)],
        out_specs=pl.BlockSpec((tt, 8, LANE), lambda i: (i, 0, 0)),
        out_shape=jax.ShapeDtypeStruct((t, 8, LANE), F32),
        scratch_shapes=[pltpu.VMEM((hk * ROW_SUB, LANE), jnp.uint32), pltpu.VMEM((hk * ROW_SUB, LANE), jnp.uint32)],
        compiler_params=_cparams(("arbitrary",), VMEM_LIMIT),
        name="peer_down",
    )(idx, act, x.reshape(t, 8, LANE), gate2.reshape(-1, 8, LANE), tab_v)
    return out.reshape(t, d)


def peer_ffn(x, g, sc, sh, gate2, wq, keys, tab_u, tab_v, seq):
    qry, h = norm_linear(x, g, sc, sh, wq, jnp.zeros((wq.shape[1],), F32), seq, emit_h=True)
    eid_t, gate_t = peer_route(qry, keys)
    idx = eid_t.T
    act = peer_up(idx, h, gate_t.T, tab_u)
    return peer_down(idx, act, x, gate2, tab_v, seq)


def _even_mixer(xt, g, sc, sh, gate, w_in, conv_w, igate_b, fgate_b, mnorm_g, qn_g, kn_g, w_out, bsz, seq):
    d = xt.shape[1]
    n_gate = 2 * M_HEADS
    g0 = 4 * M_W
    w_main = jnp.concatenate([w_in[:, :g0], w_in[:, g0 + n_gate:]], axis=1).astype(BF16)
    w_gate = jnp.pad(w_in[:, g0:g0 + n_gate], ((0, 0), (0, LANE - n_gate)))
    b_gate = jnp.pad(jnp.concatenate([igate_b, fgate_b]), (0, LANE - n_gate))
    z = norm_linear(xt, g, sc, sh, w_main, jnp.zeros((w_main.shape[1],), F32), seq)
    zg = norm_linear(xt, g, sc, sh, w_gate, b_gate, seq)
    icol, fcol, irow, frow = split_gates(zg)
    hm = mlstm_mixer(z, conv_w, icol, fcol, irow, frow, mnorm_g, bsz, seq)
    blk = g0 // LANE
    nblk = B_W // LANE
    qn, kn, kmean = head_norm(z, blk, blk + nblk, nblk, qn_g, kn_g, True)
    hb = moba_attention(qn, kn, kmean, z, blk + 2 * nblk, bsz, seq)
    w_out = w_out.astype(BF16)
    return linear_residual([hm, hb], [w_out[:M_W], w_out[M_W:]], xt, gate, seq)


def _odd_mixer(xt, g, sc, sh, gate, w_in, qn_g, kn_g, lam_p, onorm_g, w_out, lam_init, bsz, seq):
    z = norm_linear(xt, g, sc, sh, w_in.astype(BF16), jnp.zeros((w_in.shape[1],), F32), seq)
    nblk = 2 * C_HEADS * C_DH // LANE
    qn, kn = head_norm(z, 0, nblk, nblk, qn_g, kn_g, False)
    o = diff_attention(qn, kn, z, 2 * nblk, lam_p, onorm_g, lam_init, bsz, seq)
    return linear_residual([o], [w_out.astype(BF16)], xt, gate, seq)


def kernel(x, c, ada_w, ada_b, norm_mix_g, norm_ffn_g, ev_w_in, ev_conv_w, ev_igate_b, ev_fgate_b, ev_mnorm_g,
           ev_qn_g, ev_kn_g, ev_w_out, od_w_in, od_qn_g, od_kn_g, od_lam, od_onorm_g, od_w_out, peer_wq,
           peer_keys, peer_u, peer_v):
    bsz, seq, d = x.shape
    depth = ada_w.shape[0]
    mod = adaln_mod(c, ada_w, ada_b).reshape(depth, bsz, 6, 1, d)
    xt = x.reshape(bsz * seq, d)
    for layer in range(depth):
        sh1, sc1, g1, sh2, sc2, g2 = [mod[layer, :, i] for i in range(6)]
        if layer % 2 == 0:
            e = layer // 2
            xt = _even_mixer(xt, norm_mix_g[layer], sc1, sh1, g1, ev_w_in[e], ev_conv_w[e], ev_igate_b[e],
                             ev_fgate_b[e], ev_mnorm_g[e], ev_qn_g[e], ev_kn_g[e], ev_w_out[e], bsz, seq)
        else:
            o = layer // 2
            lam_init = 0.8 - 0.6 * math.exp(-0.3 * layer)
            xt = _odd_mixer(xt, norm_mix_g[layer], sc1, sh1, g1, od_w_in[o], od_qn_g[o], od_kn_g[o], od_lam[o],
                            od_onorm_g[o], od_w_out[o], lam_init, bsz, seq)
        xt = peer_ffn(xt, norm_ffn_g[layer], sc2, sh2, g2, peer_wq[layer].astype(BF16), peer_keys[layer],
                      pack_table(peer_u[layer]), pack_table(peer_v[layer]), seq)
    return xt.reshape(bsz, seq, d)
```

```python
import functools
import math

import jax
import jax.numpy as jnp
from jax import lax
from jax.experimental import pallas as pl
from jax.experimental.pallas import tpu as pltpu

F32 = jnp.float32
BF16 = jnp.bfloat16
HIGHEST = lax.Precision.HIGHEST
NEG_INF = float("-inf")

EPS = 1e-6
D_MODEL = 1024
M_HEADS, M_DH, M_CHUNK, CONV_K = 4, 128, 128, 4
M_W = M_HEADS * M_DH
B_HEADS, B_DH, MOBA_BLOCK, MOBA_TOPK = 8, 64, 256, 3
B_W = B_HEADS * B_DH
C_HEADS, C_DH, C_VDH = 8, 64, 128
PEER_HEADS, PEER_NKEYS, PEER_DK, PEER_TOPK = 8, 128, 256, 16

LANE = 128
VMEM_LIMIT = 56 * 1024 * 1024


def _cparams(sem, vmem=None):
    return pltpu.CompilerParams(dimension_semantics=sem, vmem_limit_bytes=vmem)


def _mod_kernel(c_ref, w_ref, b_ref, o_ref):
    c = c_ref[...]
    ca = c * (1.0 / (1.0 + jnp.exp(-c)))
    o_ref[...] = jnp.dot(ca, w_ref[...], precision=HIGHEST, preferred_element_type=F32) + b_ref[...]


def adaln_mod(c, ada_w, ada_b):
    depth, d, n = ada_w.shape
    bsz = c.shape[0]
    tn = 512
    return pl.pallas_call(
        _mod_kernel,
        grid=(depth, n // tn),
        in_specs=[pl.BlockSpec((bsz, d), lambda l, j: (0, 0)),
                  pl.BlockSpec((None, d, tn), lambda l, j: (l, 0, j)),
                  pl.BlockSpec((None, 1, tn), lambda l, j: (l, 0, j))],
        out_specs=pl.BlockSpec((None, bsz, tn), lambda l, j: (l, 0, j)),
        out_shape=jax.ShapeDtypeStruct((depth, bsz, n), F32),
        compiler_params=_cparams(("parallel", "parallel")),
        name="adaln_mod",
    )(c, ada_w, ada_b.reshape(depth, 1, n))


def _norm_linear_kernel(x_ref, g_ref, sc_ref, sh_ref, w_ref, b_ref, o_ref, *rest, emit_h, precise):
    if emit_h:
        h_out_ref, h_sc = rest
    else:
        (h_sc,) = rest

    @pl.when(pl.program_id(1) == 0)
    def _():
        x = x_ref[...]
        y = x * lax.rsqrt(jnp.mean(x * x, axis=-1, keepdims=True) + EPS) * g_ref[...]
        h = y * (1.0 + sc_ref[...]) + sh_ref[...]
        h_sc[...] = h.astype(h_sc.dtype)
        if emit_h:
            h_out_ref[...] = h

    if precise:
        acc = jnp.dot(h_sc[...], w_ref[...], precision=HIGHEST, preferred_element_type=F32)
    else:
        acc = jnp.dot(h_sc[...], w_ref[...], preferred_element_type=F32)
    o_ref[...] = (acc + b_ref[...]).astype(o_ref.dtype)


def norm_linear(x, g, sc, sh, w, bias, seq, *, tm=512, tn=512, emit_h=False, out_dtype=F32):
    t, d = x.shape
    n = w.shape[1]
    tn = min(tn, n)
    precise = w.dtype == F32
    rows_per_batch = seq // tm
    out_shape = [jax.ShapeDtypeStruct((t, n), out_dtype)]
    out_specs = [pl.BlockSpec((tm, tn), lambda i, j: (i, j))]
    if emit_h:
        out_shape.append(jax.ShapeDtypeStruct((t, d), F32))
        out_specs.append(pl.BlockSpec((tm, d), lambda i, j: (i, 0)))
    res = pl.pallas_call(
        functools.partial(_norm_linear_kernel, emit_h=emit_h, precise=precise),
        grid=(t // tm, n // tn),
        in_specs=[pl.BlockSpec((tm, d), lambda i, j: (i, 0)),
                  pl.BlockSpec((1, d), lambda i, j: (0, 0)),
                  pl.BlockSpec((None, 1, d), lambda i, j: (i // rows_per_batch, 0, 0)),
                  pl.BlockSpec((None, 1, d), lambda i, j: (i // rows_per_batch, 0, 0)),
                  pl.BlockSpec((d, tn), lambda i, j: (0, j)),
                  pl.BlockSpec((1, tn), lambda i, j: (0, j))],
        out_specs=out_specs,
        out_shape=out_shape,
        scratch_shapes=[pltpu.VMEM((tm, d), w.dtype)],
        compiler_params=_cparams(("parallel", "arbitrary"), VMEM_LIMIT),
        name="norm_linear",
    )(x, g.reshape(1, d), sc, sh, w, bias.reshape(1, n))
    return res if emit_h else res[0]


def _linear_residual_kernel(*refs, nparts):
    y_refs, w_refs = refs[:nparts], refs[nparts:2 * nparts]
    x_ref, gate_ref, o_ref = refs[2 * nparts:]
    acc = None
    for y_ref, w_ref in zip(y_refs, w_refs):
        part = jnp.dot(y_ref[...].astype(BF16), w_ref[...], preferred_element_type=F32)
        acc = part if acc is None else acc + part
    o_ref[...] = x_ref[...] + gate_ref[...] * acc


def linear_residual(ys, ws, x, gate, seq, *, tm=512):
    t, d = x.shape
    rows_per_batch = seq // tm
    return pl.pallas_call(
        functools.partial(_linear_residual_kernel, nparts=len(ys)),
        grid=(t // tm,),
        in_specs=([pl.BlockSpec((tm, y.shape[1]), lambda i: (i, 0)) for y in ys]
                  + [pl.BlockSpec(w.shape, lambda i: (0, 0)) for w in ws]
                  + [pl.BlockSpec((tm, d), lambda i: (i, 0)),
                     pl.BlockSpec((None, 1, d), lambda i: (i // rows_per_batch, 0, 0))]),
        out_specs=pl.BlockSpec((tm, d), lambda i: (i, 0)),
        out_shape=jax.ShapeDtypeStruct((t, d), F32),
        compiler_params=_cparams(("parallel",), VMEM_LIMIT),
        name="linear_residual",
    )(*ys, *ws, x, gate)


def _shift_rows(x, tail, s):
    if s == 0:
        return x
    xs = pltpu.roll(x, s, axis=0)
    ts = pltpu.roll(tail, s, axis=0)
    row = lax.broadcasted_iota(jnp.int32, (8, x.shape[1]), 0)
    top = jnp.where(row < s, ts, xs[0:8])
    return jnp.concatenate([top, xs[8:]], axis=0)


def _conv_silu(x, tail, w):
    acc = None
    for j in range(CONV_K):
        term = w[j:j + 1, :] * _shift_rows(x, tail, CONV_K - 1 - j)
        acc = term if acc is None else acc + term
    return acc * (1.0 / (1.0 + jnp.exp(-acc)))


def _log_sigmoid(x):
    return jnp.minimum(x, 0.0) - jnp.log(1.0 + jnp.exp(-jnp.abs(x)))


def _mlstm_kernel(q_ref, k_ref, v_ref, o_ref, wq_ref, wk_ref, icol_ref, fcol_ref, irow_ref, frow_ref,
                  g_ref, out_ref, c_sc, n_sc, m_sc, qt_sc, kt_sc):
    L = M_CHUNK

    @pl.when(pl.program_id(2) == 0)
    def _():
        c_sc[...] = jnp.zeros_like(c_sc)
        n_sc[...] = jnp.zeros_like(n_sc)
        m_sc[...] = jnp.zeros_like(m_sc)
        qt_sc[...] = jnp.zeros_like(qt_sc)
        kt_sc[...] = jnp.zeros_like(kt_sc)

    q_raw = q_ref[...]
    k_raw = k_ref[...]
    q = _conv_silu(q_raw, qt_sc[...], wq_ref[...])
    k = _conv_silu(k_raw, kt_sc[...], wk_ref[...]) * (M_DH ** -0.5)
    qt_sc[...] = q_raw[L - 8:, :]
    kt_sc[...] = k_raw[L - 8:, :]
    v = v_ref[...]

    row = lax.broadcasted_iota(jnp.int32, (L, L), 0)
    col = lax.broadcasted_iota(jnp.int32, (L, L), 1)
    causal = row >= col
    tri = jnp.where(causal, 1.0, 0.0).astype(F32)
    triu = jnp.where(row <= col, 1.0, 0.0).astype(F32)

    lf_col = _log_sigmoid(fcol_ref[...])
    lf_row = _log_sigmoid(frow_ref[...])
    ig_col = icol_ref[...]
    ig_row = irow_ref[...]
    b_col = jnp.dot(tri, jnp.broadcast_to(lf_col, (L, L)), precision=HIGHEST,
                    preferred_element_type=F32)[:, 0:1]
    b_row = jnp.dot(jnp.broadcast_to(lf_row, (8, L)), triu, precision=HIGHEST,
                    preferred_element_type=F32)[0:1, :]
    b_end = b_row[:, L - 1:L]

    m_prev = m_sc[...]
    c_prev = c_sc[...]
    n_prev = n_sc[...]

    log_d = jnp.where(causal, b_col - b_row + ig_row, NEG_INF)
    a_inter = b_col + m_prev
    m_t = jnp.maximum(a_inter, jnp.max(log_d, axis=1, keepdims=True))
    w_inter = jnp.exp(a_inter - m_t)
    qb = q.astype(BF16)
    kb = k.astype(BF16)
    vb = v.astype(BF16)
    s = lax.dot_general(qb, kb, (((1,), (1,)), ((), ())), preferred_element_type=F32)
    s = s * jnp.exp(log_d - m_t)
    num = (w_inter * jnp.dot(qb, c_prev.astype(BF16), preferred_element_type=F32)
           + jnp.dot(s.astype(BF16), vb, preferred_element_type=F32))
    den = (w_inter * jnp.sum(q * n_prev, axis=1, keepdims=True) + jnp.sum(s, axis=1, keepdims=True))
    h = num / jnp.maximum(jnp.abs(den), jnp.exp(-m_t))

    w_end_row = b_end - b_row + ig_row
    m_loc = jnp.max(w_end_row, axis=1, keepdims=True)
    e_col = jnp.exp(b_end - b_col + ig_col - m_loc)
    ke = k * e_col
    c_loc = lax.dot_general(ke.astype(BF16), vb, (((0,), (0,)), ((), ())), preferred_element_type=F32)
    n_loc = jnp.sum(ke, axis=0, keepdims=True)
    m_new = jnp.maximum(b_end + m_prev, m_loc)
    a = jnp.exp(b_end + m_prev - m_new)
    sc = jnp.exp(m_loc - m_new)
    c_sc[...] = a * c_prev + sc * c_loc
    n_sc[...] = a * n_prev + sc * n_loc
    m_sc[...] = m_new

    hn = h * lax.rsqrt(jnp.mean(h * h, axis=1, keepdims=True) + EPS) * g_ref[...]
    og = o_ref[...]
    out_ref[...] = (hn * (1.0 / (1.0 + jnp.exp(-og)))).astype(out_ref.dtype)


def mlstm_mixer(z, conv_w, icol, fcol, irow, frow, mnorm_g, bsz, seq):
    t = bsz * seq
    L = M_CHUNK
    nc = seq // L
    hq = M_W // LANE

    def zspec(sec):
        return pl.BlockSpec((L, M_DH), lambda b, h, c, sec=sec: (b * nc + c, sec * hq + h))

    return pl.pallas_call(
        _mlstm_kernel,
        grid=(bsz, M_HEADS, nc),
        in_specs=[zspec(0), zspec(1), zspec(2), zspec(3),
                  pl.BlockSpec((CONV_K, M_DH), lambda b, h, c: (0, h)),
                  pl.BlockSpec((CONV_K, M_DH), lambda b, h, c: (0, hq + h)),
                  pl.BlockSpec((None, L, 1), lambda b, h, c: (h, b * nc + c, 0)),
                  pl.BlockSpec((None, L, 1), lambda b, h, c: (h, b * nc + c, 0)),
                  pl.BlockSpec((None, None, 1, L), lambda b, h, c: (h, b * nc + c, 0, 0)),
                  pl.BlockSpec((None, None, 1, L), lambda b, h, c: (h, b * nc + c, 0, 0)),
                  pl.BlockSpec((None, 1, M_DH), lambda b, h, c: (h, 0, 0))],
        out_specs=pl.BlockSpec((L, M_DH), lambda b, h, c: (b * nc + c, h)),
        out_shape=jax.ShapeDtypeStruct((t, M_W), F32),
        scratch_shapes=[pltpu.VMEM((M_DH, M_DH), F32), pltpu.VMEM((1, M_DH), F32), pltpu.VMEM((1, 1), F32),
                        pltpu.VMEM((8, M_DH), F32), pltpu.VMEM((8, M_DH), F32)],
        compiler_params=_cparams(("parallel", "parallel", "arbitrary")),
        name="mlstm",
    )(z, z, z, z, conv_w, conv_w, icol, fcol, irow, frow, mnorm_g.reshape(M_HEADS, 1, M_DH))


def split_gates(zg):
    t = zg.shape[0]
    gt = zg[:, :2 * M_HEADS].T
    cols = gt.reshape(2 * M_HEADS, t, 1)
    rows = gt.reshape(2 * M_HEADS, t // M_CHUNK, 1, M_CHUNK)
    return cols[:M_HEADS], cols[M_HEADS:], rows[:M_HEADS], rows[M_HEADS:]


ATT_BLOCK = 256
HALF = 64


def _half_rms_norm(x, g):
    lo = lax.broadcasted_iota(jnp.int32, x.shape, 1) < HALF
    x2 = x * x
    s_lo = jnp.sum(jnp.where(lo, x2, 0.0), axis=1, keepdims=True)
    s_hi = jnp.sum(jnp.where(lo, 0.0, x2), axis=1, keepdims=True)
    r = jnp.where(lo, lax.rsqrt(s_lo * (1.0 / HALF) + EPS), lax.rsqrt(s_hi * (1.0 / HALF) + EPS))
    return x * r * g


def _head_norm_kernel(q_ref, k_ref, gq_ref, gk_ref, qn_ref, kn_ref, *rest):
    qn_ref[...] = _half_rms_norm(q_ref[...], gq_ref[...])
    kn = _half_rms_norm(k_ref[...], gk_ref[...])
    kn_ref[...] = kn.astype(kn_ref.dtype)
    if rest:
        rest[0][...] = jnp.mean(kn, axis=0, keepdims=True)


def head_norm(z, q_blk0, k_blk0, nblk, gq, gk, with_kmean):
    t = z.shape[0]
    w = nblk * LANE
    tb = ATT_BLOCK
    gq2 = jnp.concatenate([gq, gq]).reshape(1, LANE)
    gk2 = jnp.concatenate([gk, gk]).reshape(1, LANE)
    out_shape = [jax.ShapeDtypeStruct((t, w), F32), jax.ShapeDtypeStruct((t, w), BF16)]
    out_specs = [pl.BlockSpec((tb, LANE), lambda i, j: (i, j)), pl.BlockSpec((tb, LANE), lambda i, j: (i, j))]
    if with_kmean:
        out_shape.append(jax.ShapeDtypeStruct((t // tb, 1, w), F32))
        out_specs.append(pl.BlockSpec((None, 1, LANE), lambda i, j: (i, 0, j)))
    return pl.pallas_call(
        _head_norm_kernel,
        grid=(t // tb, nblk),
        in_specs=[pl.BlockSpec((tb, LANE), lambda i, j: (i, q_blk0 + j)),
                  pl.BlockSpec((tb, LANE), lambda i, j: (i, k_blk0 + j)),
                  pl.BlockSpec((1, LANE), lambda i, j: (0, 0)),
                  pl.BlockSpec((1, LANE), lambda i, j: (0, 0))],
        out_specs=out_specs,
        out_shape=out_shape,
        compiler_params=_cparams(("parallel", "parallel")),
        name="head_norm",
    )(z, z, gq2, gk2)


def _attn_kernel(*refs, moba, lam_init):
    if moba:
        (slope_ref, q_ref, k_ref, vt_ref, km_ref, out_ref, sel_sc, m_sc, l_sc, acc_sc) = refs
    else:
        (slope_ref, q_ref, k_ref, vt_ref, lam_ref, g_ref, out_ref, m_sc, l_sc, acc_sc) = refs
    tb = ATT_BLOCK
    grp = pl.program_id(1)
    j = pl.program_id(2)
    scale = HALF ** -0.5

    q = q_ref[...]
    lo = lax.broadcasted_iota(jnp.int32, (tb, LANE), 1) < HALF
    qs = q * scale
    q_half = [jnp.where(lo, qs, 0.0).astype(BF16), jnp.where(lo, 0.0, qs).astype(BF16)]
    krow = lax.broadcasted_iota(jnp.int32, (tb, tb), 0)
    causal = krow <= lax.broadcasted_iota(jnp.int32, (tb, tb), 1)
    if moba:
        slopes = [slope_ref[2 * grp], slope_ref[2 * grp + 1]]
    else:
        slopes = [slope_ref[grp], slope_ref[grp]]
    kpos = krow.astype(F32)
    bias = [slopes[0] * kpos, slopes[1] * kpos] if moba else [slopes[0] * kpos] * 2

    if moba:
        km = km_ref[...]
        blk = lax.broadcasted_iota(jnp.int32, (LANE, tb), 0)
        for hf in range(2):
            qm = jnp.where(lo, q, 0.0) if hf == 0 else jnp.where(lo, 0.0, q)
            gate = lax.dot_general(km, qm, (((1,), (1,)), ((), ())), precision=HIGHEST,
                                   preferred_element_type=F32)
            gate = jnp.where(blk < j, gate, NEG_INF)
            sel = jnp.zeros((LANE, tb), F32)
            for _ in range(MOBA_TOPK):
                mx = jnp.max(gate, axis=0, keepdims=True)
                idx = jnp.min(jnp.where(gate == mx, blk, LANE), axis=0, keepdims=True)
                hit = blk == idx
                sel = jnp.where(jnp.logical_and(hit, mx > NEG_INF), 1.0, sel)
                gate = jnp.where(hit, NEG_INF, gate)
            sel_sc[hf] = sel

    m_sc[...] = jnp.full(m_sc.shape, NEG_INF, F32)
    l_sc[...] = jnp.zeros(l_sc.shape, F32)
    acc_sc[...] = jnp.zeros(acc_sc.shape, F32)

    def block_update(blocks, diag):
        m_old = [m_sc[0], m_sc[1]]
        l_old = [l_sc[0], l_sc[1]]
        acc_old = [acc_sc[0], acc_sc[1]]
        xs, shifts, vts = [], [], []
        for n in blocks:
            start = pl.multiple_of(n * tb, tb)
            kb = k_ref[pl.ds(start, tb), :]
            vts.append(vt_ref[n])
            off = ((n - j) * tb).astype(F32)
            for hf in range(2):
                x = lax.dot_general(kb, q_half[hf], (((1,), (1,)), ((), ())), preferred_element_type=F32)
                x = x + bias[hf]
                if diag:
                    x = jnp.where(causal, x, NEG_INF)
                elif moba:
                    x = jnp.where(sel_sc[hf, pl.ds(n, 1), :] > 0.5, x, NEG_INF)
                xs.append(x)
                shifts.append(slopes[hf] * off)
        for hf in range(2):
            ids = [2 * b + hf for b in range(len(blocks))]
            m_new = m_old[hf]
            for i in ids:
                m_new = jnp.maximum(m_new, jnp.max(xs[i], axis=0, keepdims=True) + shifts[i])
            alpha = jnp.exp(m_old[hf] - m_new)
            l_new = alpha * l_old[hf]
            acc_new = alpha * acc_old[hf]
            for b, i in enumerate(ids):
                p = jnp.exp(xs[i] - (m_new - shifts[i]))
                l_new = l_new + jnp.sum(p, axis=0, keepdims=True)
                acc_new = acc_new + jnp.dot(vts[b], p.astype(BF16), preferred_element_type=F32)
            l_sc[hf] = l_new
            acc_sc[hf] = acc_new
            m_sc[hf] = m_new

    block_update([j], True)

    def body(i, carry):
        block_update([2 * i, 2 * i + 1], False)
        return carry

    lax.fori_loop(0, j // 2, body, 0)

    @pl.when(j % 2 == 1)
    def _():
        block_update([j - 1], False)

    o0 = acc_sc[0] / l_sc[0]
    o1 = acc_sc[1] / l_sc[1]
    if moba:
        first_head = lax.broadcasted_iota(jnp.int32, (LANE, tb), 0) < HALF
        out_ref[...] = jnp.where(first_head, o0, o1).T.astype(out_ref.dtype)
    else:
        lp = lam_ref[...]
        s1 = jnp.sum(lp[0:1] * lp[1:2], axis=1, keepdims=True)
        s2 = jnp.sum(lp[2:3] * lp[3:4], axis=1, keepdims=True)
        lam = jnp.exp(s1) - jnp.exp(s2) + lam_init
        o = (o0 - lam * o1).T
        o = o * lax.rsqrt(jnp.mean(o * o, axis=1, keepdims=True) + EPS) * g_ref[...]
        out_ref[...] = (o * (1.0 - lam_init)).astype(out_ref.dtype)


def _alibi_slopes(n_heads):
    return jnp.asarray([2.0 ** (-8.0 * (h + 1) / n_heads) for h in range(n_heads)], dtype=F32)


def _value_blocks_t(z, v_blk0, ngrp, bsz, seq):
    nb = seq // ATT_BLOCK
    v = z[:, v_blk0 * LANE:(v_blk0 + ngrp) * LANE].astype(BF16)
    return v.reshape(bsz, nb, ATT_BLOCK, ngrp, LANE).transpose(0, 3, 1, 4, 2)


def _attn_scratch(tb):
    return [pltpu.VMEM((2, 1, tb), F32), pltpu.VMEM((2, 1, tb), F32), pltpu.VMEM((2, LANE, tb), F32)]


def moba_attention(qn, kn, kmean, z, v_blk0, bsz, seq):
    t = bsz * seq
    tb = ATT_BLOCK
    nb = seq // tb
    ngrp = B_W // LANE
    km = jnp.pad(kmean.reshape(bsz, nb, B_W), ((0, 0), (0, LANE - nb), (0, 0)))
    return pl.pallas_call(
        functools.partial(_attn_kernel, moba=True, lam_init=None),
        grid=(bsz, ngrp, nb),
        in_specs=[pl.BlockSpec(memory_space=pltpu.SMEM),
                  pl.BlockSpec((tb, LANE), lambda b, g, j: (b * nb + j, g)),
                  pl.BlockSpec((seq, LANE), lambda b, g, j: (b, g)),
                  pl.BlockSpec((None, None, nb, LANE, tb), lambda b, g, j: (b, g, 0, 0, 0)),
                  pl.BlockSpec((None, LANE, LANE), lambda b, g, j: (b, 0, g))],
        out_specs=pl.BlockSpec((tb, LANE), lambda b, g, j: (b * nb + j, g)),
        out_shape=jax.ShapeDtypeStruct((t, B_W), F32),
        scratch_shapes=[pltpu.VMEM((2, LANE, tb), F32)] + _attn_scratch(tb),
        compiler_params=_cparams(("parallel", "parallel", "arbitrary"), VMEM_LIMIT),
        name="moba_attention",
    )(_alibi_slopes(B_HEADS), qn, kn, _value_blocks_t(z, v_blk0, ngrp, bsz, seq), km)


def diff_attention(qn, kn, z, v_blk0, lam_p, onorm_g, lam_init, bsz, seq):
    t = bsz * seq
    tb = ATT_BLOCK
    nb = seq // tb
    return pl.pallas_call(
        functools.partial(_attn_kernel, moba=False, lam_init=lam_init),
        grid=(bsz, C_HEADS, nb),
        in_specs=[pl.BlockSpec(memory_space=pltpu.SMEM),
                  pl.BlockSpec((tb, LANE), lambda b, g, j: (b * nb + j, g)),
                  pl.BlockSpec((seq, LANE), lambda b, g, j: (b, g)),
                  pl.BlockSpec((None, None, nb, LANE, tb), lambda b, g, j: (b, g, 0, 0, 0)),
                  pl.BlockSpec((4, C_DH), lambda b, g, j: (0, 0)),
                  pl.BlockSpec((1, C_VDH), lambda b, g, j: (0, 0))],
        out_specs=pl.BlockSpec((tb, LANE), lambda b, g, j: (b * nb + j, g)),
        out_shape=jax.ShapeDtypeStruct((t, C_HEADS * C_VDH), F32),
        scratch_shapes=_attn_scratch(tb),
        compiler_params=_cparams(("parallel", "parallel", "arbitrary"), VMEM_LIMIT),
        name="diff_attention",
    )(_alibi_slopes(C_HEADS), qn, kn, _value_blocks_t(z, v_blk0, C_HEADS, bsz, seq), lam_p,
      onorm_g.reshape(1, C_VDH))


def _top_rows(s, count):
    n = s.shape[0]
    rows = lax.broadcasted_iota(jnp.int32, s.shape, 0)
    vals, idxs = [], []
    for _ in range(count):
        mx = jnp.max(s, axis=0, keepdims=True)
        ix = jnp.min(jnp.where(s == mx, rows, n), axis=0, keepdims=True)
        vals.append(mx)
        idxs.append(ix)
        s = jnp.where(rows == ix, NEG_INF, s)
    return jnp.concatenate(vals, axis=0), jnp.concatenate(idxs, axis=0)


def _peer_route_kernel(q_ref, keys_ref, eid_ref, gate_ref):
    kk = PEER_TOPK
    half = PEER_DK // 2
    tops = []
    for p in range(2):
        qp = q_ref[:, p * half:(p + 1) * half]
        st = lax.dot_general(keys_ref[p], qp, (((1,), (1,)), ((), ())), precision=HIGHEST,
                             preferred_element_type=F32)
        tops.append(_top_rows(st, kk))
    (s0, i0), (s1, i1) = tops
    cand_s = jnp.concatenate([s0[a:a + 1] + s1 for a in range(kk)], axis=0)
    cand_i = jnp.concatenate([i0[a:a + 1] * PEER_NKEYS + i1 for a in range(kk)], axis=0)
    score, pos = _top_rows(cand_s, kk)
    rows = lax.broadcasted_iota(jnp.int32, cand_i.shape, 0)
    eid = jnp.concatenate(
        [jnp.sum(jnp.where(rows == pos[a:a + 1], cand_i, 0), axis=0, keepdims=True) for a in range(kk)], axis=0)
    e = jnp.exp(score - score[0:1])
    eid_ref[...] = eid
    gate_ref[...] = e / jnp.sum(e, axis=0, keepdims=True)


def peer_route(qry, keys, *, tt=256):
    t = qry.shape[0]
    hk = PEER_HEADS * PEER_TOPK
    return pl.pallas_call(
        _peer_route_kernel,
        grid=(t // tt, PEER_HEADS),
        in_specs=[pl.BlockSpec((tt, PEER_DK), lambda i, h: (i, h)),
                  pl.BlockSpec((2, PEER_NKEYS, PEER_DK // 2), lambda i, h: (0, 0, 0))],
        out_specs=[pl.BlockSpec((PEER_TOPK, tt), lambda i, h: (h, i)),
                   pl.BlockSpec((PEER_TOPK, tt), lambda i, h: (h, i))],
        out_shape=[jax.ShapeDtypeStruct((hk, t), jnp.int32), jax.ShapeDtypeStruct((hk, t), F32)],
        compiler_params=_cparams(("parallel", "parallel")),
        name="peer_route",
    )(qry, keys)


ROW_SUB = 4
PEER_TT = 64


PEER_HK = PEER_HEADS * PEER_TOPK
ROW_BF = 2 * ROW_SUB
GROUP = 8


def pack_table(tab):
    n, d = tab.shape
    b = lax.bitcast_convert_type(tab.astype(BF16), jnp.uint16).astype(jnp.uint32)
    b = b.reshape(n, ROW_SUB, 2, LANE)
    return b[:, :, 0, :] | (b[:, :, 1, :] << 16)


def _split_bf16(x):
    hi = x.astype(BF16)
    lo = (x - hi.astype(F32)).astype(BF16)
    return jnp.concatenate([hi, lo], axis=0)


def _gelu(x):
    return 0.5 * x * (1.0 + lax.erf(x * (2.0 ** -0.5)))


def _gather_rows(idx_ref, tab_ref, dst_ref, t):
    for k in range(PEER_HK):
        dst_ref[k * ROW_SUB:(k + 1) * ROW_SUB, :] = tab_ref[idx_ref[t, k]]


def _diag_mask(rows):
    shape = (rows, PEER_HK * ROW_BF)
    return (lax.broadcasted_iota(jnp.int32, shape, 1) % ROW_BF) == (lax.broadcasted_iota(jnp.int32, shape, 0) % ROW_BF)


def _peer_up_kernel(idx_ref, h_ref, gate_ref, tab_ref, act_ref, g0_sc, g1_sc, c_sc):
    slots = (g0_sc, g1_sc)
    wide = PEER_HK * ROW_BF
    mask = _diag_mask(2 * ROW_BF)
    fold = (lax.broadcasted_iota(jnp.int32, (wide, PEER_HK), 0) // ROW_BF
            == lax.broadcasted_iota(jnp.int32, (wide, PEER_HK), 1)).astype(BF16)

    def group(g, carry):
        t0 = g * GROUP
        _gather_rows(idx_ref, tab_ref, slots[0], t0)
        for i in range(GROUP):
            if i + 1 < GROUP:
                _gather_rows(idx_ref, tab_ref, slots[(i + 1) % 2], t0 + i + 1)
            rows = pltpu.bitcast(slots[i % 2][...], BF16)
            xs = _split_bf16(h_ref[t0 + i])
            y = lax.dot_general(xs, rows, (((1,), (1,)), ((), ())), preferred_element_type=F32)
            y = jnp.where(mask, y, 0.0)
            c_sc[i:i + 1, :] = jnp.sum(y, axis=0, keepdims=True)
        s = jnp.dot(_split_bf16(c_sc[...]), fold, preferred_element_type=F32)
        s = s[0:GROUP] + s[GROUP:]
        rows8 = pl.ds(pl.multiple_of(t0, GROUP), GROUP)
        act_ref[rows8, :] = _gelu(s) * gate_ref[rows8, :]
        return carry

    lax.fori_loop(0, PEER_TT // GROUP, group, 0)


def _peer_down_kernel(idx_ref, act_ref, x_ref, g_ref, tab_ref, out_ref, g0_sc, g1_sc):
    slots = (g0_sc, g1_sc)
    wide = PEER_HK * ROW_BF
    mask = _diag_mask(ROW_BF)
    spread = (lax.broadcasted_iota(jnp.int32, (PEER_HK, wide), 1) // ROW_BF
              == lax.broadcasted_iota(jnp.int32, (PEER_HK, wide), 0)).astype(BF16)
    gate = g_ref[...]

    def group(g, carry):
        t0 = g * GROUP
        _gather_rows(idx_ref, tab_ref, slots[0], t0)
        acts = act_ref[pl.ds(pl.multiple_of(t0, GROUP), GROUP), :]
        a_wide = jnp.dot(_split_bf16(acts), spread, preferred_element_type=F32)
        for i in range(GROUP):
            if i + 1 < GROUP:
                _gather_rows(idx_ref, tab_ref, slots[(i + 1) % 2], t0 + i + 1)
            rows = pltpu.bitcast(slots[i % 2][...], BF16)
            a_hi = jnp.where(mask, jnp.broadcast_to(a_wide[i:i + 1], (ROW_BF, wide)), 0.0)
            a_lo = jnp.where(mask, jnp.broadcast_to(a_wide[GROUP + i:GROUP + i + 1], (ROW_BF, wide)), 0.0)
            lhs = jnp.concatenate([a_hi, a_lo], axis=0).astype(BF16)
            y = jnp.dot(lhs, rows, preferred_element_type=F32)
            out_ref[t0 + i] = x_ref[t0 + i] + gate * (y[0:ROW_BF] + y[ROW_BF:])
        return carry

    lax.fori_loop(0, PEER_TT // GROUP, group, 0)


def _table_spec(n):
    return pl.BlockSpec((n, ROW_SUB, LANE), lambda i: (0, 0, 0), pipeline_mode=pl.Buffered(1))


def peer_up(idx, h, gate, tab_u):
    t, hk = idx.shape
    n = tab_u.shape[0]
    tt = PEER_TT
    return pl.pallas_call(
        _peer_up_kernel,
        grid=(t // tt,),
        in_specs=[pl.BlockSpec((tt, hk), lambda i: (i, 0), memory_space=pltpu.SMEM),
                  pl.BlockSpec((tt, 8, LANE), lambda i: (i, 0, 0)),
                  pl.BlockSpec((tt, hk), lambda i: (i, 0)),
                  _table_spec(n)],
        out_specs=pl.BlockSpec((tt, hk), lambda i: (i, 0)),
        out_shape=jax.ShapeDtypeStruct((t, hk), F32),
        scratch_shapes=[pltpu.VMEM((hk * ROW_SUB, LANE), jnp.uint32), pltpu.VMEM((hk * ROW_SUB, LANE), jnp.uint32),
                        pltpu.VMEM((GROUP, hk * ROW_BF), F32)],
        compiler_params=_cparams(("arbitrary",), VMEM_LIMIT),
        name="peer_up",
    )(idx, h.reshape(t, 8, LANE), gate, tab_u)


def peer_down(idx, act, x, gate2, tab_v, seq):
    t, hk = idx.shape
    n = tab_v.shape[0]
    d = x.shape[1]
    tt = PEER_TT
    tiles_per_batch = seq // tt
    out = pl.pallas_call(
        _peer_down_kernel,
        grid=(t // tt,),
        in_specs=[pl.BlockSpec((tt, hk), lambda i: (i, 0), memory_space=pltpu.SMEM),
                  pl.BlockSpec((tt, hk), lambda i: (i, 0)),
                  pl.BlockSpec((tt, 8, LANE), lambda i: (i, 0, 0)),
                  pl.BlockSpec((None, 8, LANE), lambda i: (i // tiles_per_batch, 0, 0)),
                  _table_spec(n)],
        out_specs=pl.BlockSpec((tt, 8, LANE), lambda i: (i, 0, 0)),
        out_shape=jax.ShapeDtypeStruct((t, 8, LANE), F32),
        scratch_shapes=[pltpu.VMEM((hk * ROW_SUB, LANE), jnp.uint32), pltpu.VMEM((hk * ROW_SUB, LANE), jnp.uint32)],
        compiler_params=_cparams(("arbitrary",), VMEM_LIMIT),
        name="peer_down",
    )(idx, act, x.reshape(t, 8, LANE), gate2.reshape(-1, 8, LANE), tab_v)
    return out.reshape(t, d)


def peer_ffn(x, g, sc, sh, gate2, wq, keys, tab_u, tab_v, seq):
    qry, h = norm_linear(x, g, sc, sh, wq, jnp.zeros((wq.shape[1],), F32), seq, emit_h=True)
    eid_t, gate_t = peer_route(qry, keys)
    idx = eid_t.T
    act = peer_up(idx, h, gate_t.T, tab_u)
    return peer_down(idx, act, x, gate2, tab_v, seq)


def _even_mixer(xt, g, sc, sh, gate, w_in, conv_w, igate_b, fgate_b, mnorm_g, qn_g, kn_g, w_out, bsz, seq):
    d = xt.shape[1]
    n_gate = 2 * M_HEADS
    g0 = 4 * M_W
    w_main = jnp.concatenate([w_in[:, :g0], w_in[:, g0 + n_gate:]], axis=1).astype(BF16)
    w_gate = jnp.pad(w_in[:, g0:g0 + n_gate], ((0, 0), (0, LANE - n_gate)))
    b_gate = jnp.pad(jnp.concatenate([igate_b, fgate_b]), (0, LANE - n_gate))
    z = norm_linear(xt, g, sc, sh, w_main, jnp.zeros((w_main.shape[1],), F32), seq)
    zg = norm_linear(xt, g, sc, sh, w_gate, b_gate, seq)
    icol, fcol, irow, frow = split_gates(zg)
    hm = mlstm_mixer(z, conv_w, icol, fcol, irow, frow, mnorm_g, bsz, seq)
    blk = g0 // LANE
    nblk = B_W // LANE
    qn, kn, kmean = head_norm(z, blk, blk + nblk, nblk, qn_g, kn_g, True)
    hb = moba_attention(qn, kn, kmean, z, blk + 2 * nblk, bsz, seq)
    w_out = w_out.astype(BF16)
    return linear_residual([hm, hb], [w_out[:M_W], w_out[M_W:]], xt, gate, seq)


def _odd_mixer(xt, g, sc, sh, gate, w_in, qn_g, kn_g, lam_p, onorm_g, w_out, lam_init, bsz, seq):
    z = norm_linear(xt, g, sc, sh, w_in.astype(BF16), jnp.zeros((w_in.shape[1],), F32), seq)
    nblk = 2 * C_HEADS * C_DH // LANE
    qn, kn = head_norm(z, 0, nblk, nblk, qn_g, kn_g, False)
    o = diff_attention(qn, kn, z, 2 * nblk, lam_p, onorm_g, lam_init, bsz, seq)
    return linear_residual([o], [w_out.astype(BF16)], xt, gate, seq)


def kernel(x, c, ada_w, ada_b, norm_mix_g, norm_ffn_g, ev_w_in, ev_conv_w, ev_igate_b, ev_fgate_b, ev_mnorm_g,
           ev_qn_g, ev_kn_g, ev_w_out, od_w_in, od_qn_g, od_kn_g, od_lam, od_onorm_g, od_w_out, peer_wq,
           peer_keys, peer_u, peer_v):
    bsz, seq, d = x.shape
    depth = ada_w.shape[0]
    mod = adaln_mod(c, ada_w, ada_b).reshape(depth, bsz, 6, 1, d)
    xt = x.reshape(bsz * seq, d)
    for layer in range(depth):
        sh1, sc1, g1, sh2, sc2, g2 = [mod[layer, :, i] for i in range(6)]
        if layer % 2 == 0:
            e = layer // 2
            xt = _even_mixer(xt, norm_mix_g[layer], sc1, sh1, g1, ev_w_in[e], ev_conv_w[e], ev_igate_b[e],
                             ev_fgate_b[e], ev_mnorm_g[e], ev_qn_g[e], ev_kn_g[e], ev_w_out[e], bsz, seq)
        else:
            o = layer // 2
            lam_init = 0.8 - 0.6 * math.exp(-0.3 * layer)
            xt = _odd_mixer(xt, norm_mix_g[layer], sc1, sh1, g1, od_w_in[o], od_qn_g[o], od_kn_g[o], od_lam[o],
                            od_onorm_g[o], od_w_out[o], lam_init, bsz, seq)
        xt = peer_ffn(xt, norm_ffn_g[layer], sc2, sh2, g2, peer_wq[layer].astype(BF16), peer_keys[layer],
                      pack_table(peer_u[layer]), pack_table(peer_v[layer]), seq)
    return xt.reshape(bsz, seq, d)
```

```python
import functools
import math

import jax
import jax.numpy as jnp
from jax import lax
from jax.experimental import pallas as pl
from jax.experimental.pallas import tpu as pltpu

F32 = jnp.float32
BF16 = jnp.bfloat16
HIGHEST = lax.Precision.HIGHEST
NEG_INF = float("-inf")

EPS = 1e-6
D_MODEL = 1024
M_HEADS, M_DH, M_CHUNK, CONV_K = 4, 128, 128, 4
M_W = M_HEADS * M_DH
B_HEADS, B_DH, MOBA_BLOCK, MOBA_TOPK = 8, 64, 256, 3
B_W = B_HEADS * B_DH
C_HEADS, C_DH, C_VDH = 8, 64, 128
PEER_HEADS, PEER_NKEYS, PEER_DK, PEER_TOPK = 8, 128, 256, 16

LANE = 128
VMEM_LIMIT = 56 * 1024 * 1024


def _cparams(sem, vmem=None):
    return pltpu.CompilerParams(dimension_semantics=sem, vmem_limit_bytes=vmem)


def _mod_kernel(c_ref, w_ref, b_ref, o_ref):
    c = c_ref[...]
    ca = c * (1.0 / (1.0 + jnp.exp(-c)))
    o_ref[...] = jnp.dot(ca, w_ref[...], precision=HIGHEST, preferred_element_type=F32) + b_ref[...]


def adaln_mod(c, ada_w, ada_b):
    depth, d, n = ada_w.shape
    bsz = c.shape[0]
    tn = 512
    return pl.pallas_call(
        _mod_kernel,
        grid=(depth, n // tn),
        in_specs=[pl.BlockSpec((bsz, d), lambda l, j: (0, 0)),
                  pl.BlockSpec((None, d, tn), lambda l, j: (l, 0, j)),
                  pl.BlockSpec((None, 1, tn), lambda l, j: (l, 0, j))],
        out_specs=pl.BlockSpec((None, bsz, tn), lambda l, j: (l, 0, j)),
        out_shape=jax.ShapeDtypeStruct((depth, bsz, n), F32),
        compiler_params=_cparams(("parallel", "parallel")),
        name="adaln_mod",
    )(c, ada_w, ada_b.reshape(depth, 1, n))


def _norm_linear_kernel(x_ref, g_ref, sc_ref, sh_ref, w_ref, b_ref, o_ref, *rest, emit_h, precise):
    if emit_h:
        h_out_ref, h_sc = rest
    else:
        (h_sc,) = rest

    @pl.when(pl.program_id(1) == 0)
    def _():
        x = x_ref[...]
        y = x * lax.rsqrt(jnp.mean(x * x, axis=-1, keepdims=True) + EPS) * g_ref[...]
        h = y * (1.0 + sc_ref[...]) + sh_ref[...]
        h_sc[...] = h.astype(h_sc.dtype)
        if emit_h:
            h_out_ref[...] = h

    if precise:
        acc = jnp.dot(h_sc[...], w_ref[...], precision=HIGHEST, preferred_element_type=F32)
    else:
        acc = jnp.dot(h_sc[...], w_ref[...], preferred_element_type=F32)
    o_ref[...] = (acc + b_ref[...]).astype(o_ref.dtype)


def norm_linear(x, g, sc, sh, w, bias, seq, *, tm=512, tn=512, emit_h=False, out_dtype=F32):
    t, d = x.shape
    n = w.shape[1]
    tn = min(tn, n)
    precise = w.dtype == F32
    rows_per_batch = seq // tm
    out_shape = [jax.ShapeDtypeStruct((t, n), out_dtype)]
    out_specs = [pl.BlockSpec((tm, tn), lambda i, j: (i, j))]
    if emit_h:
        out_shape.append(jax.ShapeDtypeStruct((t, d), F32))
        out_specs.append(pl.BlockSpec((tm, d), lambda i, j: (i, 0)))
    res = pl.pallas_call(
        functools.partial(_norm_linear_kernel, emit_h=emit_h, precise=precise),
        grid=(t // tm, n // tn),
        in_specs=[pl.BlockSpec((tm, d), lambda i, j: (i, 0)),
                  pl.BlockSpec((1, d), lambda i, j: (0, 0)),
                  pl.BlockSpec((None, 1, d), lambda i, j: (i // rows_per_batch, 0, 0)),
                  pl.BlockSpec((None, 1, d), lambda i, j: (i // rows_per_batch, 0, 0)),
                  pl.BlockSpec((d, tn), lambda i, j: (0, j)),
                  pl.BlockSpec((1, tn), lambda i, j: (0, j))],
        out_specs=out_specs,
        out_shape=out_shape,
        scratch_shapes=[pltpu.VMEM((tm, d), w.dtype)],
        compiler_params=_cparams(("parallel", "arbitrary"), VMEM_LIMIT),
        name="norm_linear",
    )(x, g.reshape(1, d), sc, sh, w, bias.reshape(1, n))
    return res if emit_h else res[0]


def _linear_residual_kernel(*refs, nparts):
    y_refs, w_refs = refs[:nparts], refs[nparts:2 * nparts]
    x_ref, gate_ref, o_ref = refs[2 * nparts:]
    acc = None
    for y_ref, w_ref in zip(y_refs, w_refs):
        part = jnp.dot(y_ref[...].astype(BF16), w_ref[...], preferred_element_type=F32)
        acc = part if acc is None else acc + part
    o_ref[...] = x_ref[...] + gate_ref[...] * acc


def linear_residual(ys, ws, x, gate, seq, *, tm=512):
    t, d = x.shape
    rows_per_batch = seq // tm
    return pl.pallas_call(
        functools.partial(_linear_residual_kernel, nparts=len(ys)),
        grid=(t // tm,),
        in_specs=([pl.BlockSpec((tm, y.shape[1]), lambda i: (i, 0)) for y in ys]
                  + [pl.BlockSpec(w.shape, lambda i: (0, 0)) for w in ws]
                  + [pl.BlockSpec((tm, d), lambda i: (i, 0)),
                     pl.BlockSpec((None, 1, d), lambda i: (i // rows_per_batch, 0, 0))]),
        out_specs=pl.BlockSpec((tm, d), lambda i: (i, 0)),
        out_shape=jax.ShapeDtypeStruct((t, d), F32),
        compiler_params=_cparams(("parallel",), VMEM_LIMIT),
        name="linear_residual",
    )(*ys, *ws, x, gate)


def _shift_rows(x, tail, s):
    if s == 0:
        return x
    xs = pltpu.roll(x, s, axis=0)
    ts = pltpu.roll(tail, s, axis=0)
    row = lax.broadcasted_iota(jnp.int32, (8, x.shape[1]), 0)
    top = jnp.where(row < s, ts, xs[0:8])
    return jnp.concatenate([top, xs[8:]], axis=0)


def _conv_silu(x, tail, w):
    acc = None
    for j in range(CONV_K):
        term = w[j:j + 1, :] * _shift_rows(x, tail, CONV_K - 1 - j)
        acc = term if acc is None else acc + term
    return acc * (1.0 / (1.0 + jnp.exp(-acc)))


def _log_sigmoid(x):
    return jnp.minimum(x, 0.0) - jnp.log(1.0 + jnp.exp(-jnp.abs(x)))


def _mlstm_kernel(q_ref, k_ref, v_ref, o_ref, wq_ref, wk_ref, icol_ref, fcol_ref, irow_ref, frow_ref,
                  g_ref, out_ref, c_sc, n_sc, m_sc, qt_sc, kt_sc):
    L = M_CHUNK

    @pl.when(pl.program_id(2) == 0)
    def _():
        c_sc[...] = jnp.zeros_like(c_sc)
        n_sc[...] = jnp.zeros_like(n_sc)
        m_sc[...] = jnp.zeros_like(m_sc)
        qt_sc[...] = jnp.zeros_like(qt_sc)
        kt_sc[...] = jnp.zeros_like(kt_sc)

    q_raw = q_ref[...]
    k_raw = k_ref[...]
    q = _conv_silu(q_raw, qt_sc[...], wq_ref[...])
    k = _conv_silu(k_raw, kt_sc[...], wk_ref[...]) * (M_DH ** -0.5)
    qt_sc[...] = q_raw[L - 8:, :]
    kt_sc[...] = k_raw[L - 8:, :]
    v = v_ref[...]

    row = lax.broadcasted_iota(jnp.int32, (L, L), 0)
    col = lax.broadcasted_iota(jnp.int32, (L, L), 1)
    causal = row >= col
    tri = jnp.where(causal, 1.0, 0.0).astype(F32)
    triu = jnp.where(row <= col, 1.0, 0.0).astype(F32)

    lf_col = _log_sigmoid(fcol_ref[...])
    lf_row = _log_sigmoid(frow_ref[...])
    ig_col = icol_ref[...]
    ig_row = irow_ref[...]
    b_col = jnp.dot(tri, jnp.broadcast_to(lf_col, (L, L)), precision=HIGHEST,
                    preferred_element_type=F32)[:, 0:1]
    b_row = jnp.dot(jnp.broadcast_to(lf_row, (8, L)), triu, precision=HIGHEST,
                    preferred_element_type=F32)[0:1, :]
    b_end = b_row[:, L - 1:L]

    m_prev = m_sc[...]
    c_prev = c_sc[...]
    n_prev = n_sc[...]

    log_d = jnp.where(causal, b_col - b_row + ig_row, NEG_INF)
    a_inter = b_col + m_prev
    m_t = jnp.maximum(a_inter, jnp.max(log_d, axis=1, keepdims=True))
    w_inter = jnp.exp(a_inter - m_t)
    qb = q.astype(BF16)
    kb = k.astype(BF16)
    vb = v.astype(BF16)
    s = lax.dot_general(qb, kb, (((1,), (1,)), ((), ())), preferred_element_type=F32)
    s = s * jnp.exp(log_d - m_t)
    num = (w_inter * jnp.dot(qb, c_prev.astype(BF16), preferred_element_type=F32)
           + jnp.dot(s.astype(BF16), vb, preferred_element_type=F32))
    den = (w_inter * jnp.sum(q * n_prev, axis=1, keepdims=True) + jnp.sum(s, axis=1, keepdims=True))
    h = num / jnp.maximum(jnp.abs(den), jnp.exp(-m_t))

    w_end_row = b_end - b_row + ig_row
    m_loc = jnp.max(w_end_row, axis=1, keepdims=True)
    e_col = jnp.exp(b_end - b_col + ig_col - m_loc)
    ke = k * e_col
    c_loc = lax.dot_general(ke.astype(BF16), vb, (((0,), (0,)), ((), ())), preferred_element_type=F32)
    n_loc = jnp.sum(ke, axis=0, keepdims=True)
    m_new = jnp.maximum(b_end + m_prev, m_loc)
    a = jnp.exp(b_end + m_prev - m_new)
    sc = jnp.exp(m_loc - m_new)
    c_sc[...] = a * c_prev + sc * c_loc
    n_sc[...] = a * n_prev + sc * n_loc
    m_sc[...] = m_new

    hn = h * lax.rsqrt(jnp.mean(h * h, axis=1, keepdims=True) + EPS) * g_ref[...]
    og = o_ref[...]
    out_ref[...] = (hn * (1.0 / (1.0 + jnp.exp(-og)))).astype(out_ref.dtype)


def mlstm_mixer(z, conv_w, icol, fcol, irow, frow, mnorm_g, bsz, seq):
    t = bsz * seq
    L = M_CHUNK
    nc = seq // L
    hq = M_W // LANE

    def zspec(sec):
        return pl.BlockSpec((L, M_DH), lambda b, h, c, sec=sec: (b * nc + c, sec * hq + h))

    return pl.pallas_call(
        _mlstm_kernel,
        grid=(bsz, M_HEADS, nc),
        in_specs=[zspec(0), zspec(1), zspec(2), zspec(3),
                  pl.BlockSpec((CONV_K, M_DH), lambda b, h, c: (0, h)),
                  pl.BlockSpec((CONV_K, M_DH), lambda b, h, c: (0, hq + h)),
                  pl.BlockSpec((None, L, 1), lambda b, h, c: (h, b * nc + c, 0)),
                  pl.BlockSpec((None, L, 1), lambda b, h, c: (h, b * nc + c, 0)),
                  pl.BlockSpec((None, None, 1, L), lambda b, h, c: (h, b * nc + c, 0, 0)),
                  pl.BlockSpec((None, None, 1, L), lambda b, h, c: (h, b * nc + c, 0, 0)),
                  pl.BlockSpec((None, 1, M_DH), lambda b, h, c: (h, 0, 0))],
        out_specs=pl.BlockSpec((L, M_DH), lambda b, h, c: (b * nc + c, h)),
        out_shape=jax.ShapeDtypeStruct((t, M_W), F32),
        scratch_shapes=[pltpu.VMEM((M_DH, M_DH), F32), pltpu.VMEM((1, M_DH), F32), pltpu.VMEM((1, 1), F32),
                        pltpu.VMEM((8, M_DH), F32), pltpu.VMEM((8, M_DH), F32)],
        compiler_params=_cparams(("parallel", "parallel", "arbitrary")),
        name="mlstm",
    )(z, z, z, z, conv_w, conv_w, icol, fcol, irow, frow, mnorm_g.reshape(M_HEADS, 1, M_DH))


def split_gates(zg):
    t = zg.shape[0]
    gt = zg[:, :2 * M_HEADS].T
    cols = gt.reshape(2 * M_HEADS, t, 1)
    rows = gt.reshape(2 * M_HEADS, t // M_CHUNK, 1, M_CHUNK)
    return cols[:M_HEADS], cols[M_HEADS:], rows[:M_HEADS], rows[M_HEADS:]


ATT_BLOCK = 256
HALF = 64


def _half_rms_norm(x, g):
    lo = lax.broadcasted_iota(jnp.int32, x.shape, 1) < HALF
    x2 = x * x
    s_lo = jnp.sum(jnp.where(lo, x2, 0.0), axis=1, keepdims=True)
    s_hi = jnp.sum(jnp.where(lo, 0.0, x2), axis=1, keepdims=True)
    r = jnp.where(lo, lax.rsqrt(s_lo * (1.0 / HALF) + EPS), lax.rsqrt(s_hi * (1.0 / HALF) + EPS))
    return x * r * g


def _head_norm_kernel(q_ref, k_ref, gq_ref, gk_ref, qn_ref, kn_ref, *rest):
    qn_ref[...] = _half_rms_norm(q_ref[...], gq_ref[...])
    kn = _half_rms_norm(k_ref[...], gk_ref[...])
    kn_ref[...] = kn.astype(kn_ref.dtype)
    if rest:
        rest[0][...] = jnp.mean(kn, axis=0, keepdims=True)


def head_norm(z, q_blk0, k_blk0, nblk, gq, gk, with_kmean):
    t = z.shape[0]
    w = nblk * LANE
    tb = ATT_BLOCK
    gq2 = jnp.concatenate([gq, gq]).reshape(1, LANE)
    gk2 = jnp.concatenate([gk, gk]).reshape(1, LANE)
    out_shape = [jax.ShapeDtypeStruct((t, w), F32), jax.ShapeDtypeStruct((t, w), BF16)]
    out_specs = [pl.BlockSpec((tb, LANE), lambda i, j: (i, j)), pl.BlockSpec((tb, LANE), lambda i, j: (i, j))]
    if with_kmean:
        out_shape.append(jax.ShapeDtypeStruct((t // tb, 1, w), F32))
        out_specs.append(pl.BlockSpec((None, 1, LANE), lambda i, j: (i, 0, j)))
    return pl.pallas_call(
        _head_norm_kernel,
        grid=(t // tb, nblk),
        in_specs=[pl.BlockSpec((tb, LANE), lambda i, j: (i, q_blk0 + j)),
                  pl.BlockSpec((tb, LANE), lambda i, j: (i, k_blk0 + j)),
                  pl.BlockSpec((1, LANE), lambda i, j: (0, 0)),
                  pl.BlockSpec((1, LANE), lambda i, j: (0, 0))],
        out_specs=out_specs,
        out_shape=out_shape,
        compiler_params=_cparams(("parallel", "parallel")),
        name="head_norm",
    )(z, z, gq2, gk2)


def _attn_kernel(*refs, moba, lam_init):
    if moba:
        (slope_ref, q_ref, k_ref, vt_ref, km_ref, out_ref, sel_sc, m_sc, l_sc, acc_sc) = refs
    else:
        (slope_ref, q_ref, k_ref, vt_ref, lam_ref, g_ref, out_ref, m_sc, l_sc, acc_sc) = refs
    tb = ATT_BLOCK
    grp = pl.program_id(1)
    j = pl.program_id(2)
    scale = HALF ** -0.5

    q = q_ref[...]
    lo = lax.broadcasted_iota(jnp.int32, (tb, LANE), 1) < HALF
    qs = q * scale
    q_half = [jnp.where(lo, qs, 0.0).astype(BF16), jnp.where(lo, 0.0, qs).astype(BF16)]
    krow = lax.broadcasted_iota(jnp.int32, (tb, tb), 0)
    causal = krow <= lax.broadcasted_iota(jnp.int32, (tb, tb), 1)
    if moba:
        slopes = [slope_ref[2 * grp], slope_ref[2 * grp + 1]]
    else:
        slopes = [slope_ref[grp], slope_ref[grp]]
    kpos = krow.astype(F32)
    bias = [slopes[0] * kpos, slopes[1] * kpos] if moba else [slopes[0] * kpos] * 2

    if moba:
        km = km_ref[...]
        blk = lax.broadcasted_iota(jnp.int32, (LANE, tb), 0)
        for hf in range(2):
            qm = jnp.where(lo, q, 0.0) if hf == 0 else jnp.where(lo, 0.0, q)
            gate = lax.dot_general(km, qm, (((1,), (1,)), ((), ())), precision=HIGHEST,
                                   preferred_element_type=F32)
            gate = jnp.where(blk < j, gate, NEG_INF)
            sel = jnp.zeros((LANE, tb), F32)
            for _ in range(MOBA_TOPK):
                mx = jnp.max(gate, axis=0, keepdims=True)
                idx = jnp.min(jnp.where(gate == mx, blk, LANE), axis=0, keepdims=True)
                hit = blk == idx
                sel = jnp.where(jnp.logical_and(hit, mx > NEG_INF), 1.0, sel)
                gate = jnp.where(hit, NEG_INF, gate)
            sel_sc[hf] = sel

    m_sc[...] = jnp.full(m_sc.shape, NEG_INF, F32)
    l_sc[...] = jnp.zeros(l_sc.shape, F32)
    acc_sc[...] = jnp.zeros(acc_sc.shape, F32)

    def block_update(blocks, diag):
        m_old = [m_sc[0], m_sc[1]]
        l_old = [l_sc[0], l_sc[1]]
        acc_old = [acc_sc[0], acc_sc[1]]
        xs, shifts, vts = [], [], []
        for n in blocks:
            start = pl.multiple_of(n * tb, tb)
            kb = k_ref[pl.ds(start, tb), :]
            vts.append(vt_ref[n])
            off = ((n - j) * tb).astype(F32)
            for hf in range(2):
                x = lax.dot_general(kb, q_half[hf], (((1,), (1,)), ((), ())), preferred_element_type=F32)
                x = x + bias[hf]
                if diag:
                    x = jnp.where(causal, x, NEG_INF)
                elif moba:
                    x = jnp.where(sel_sc[hf, pl.ds(n, 1), :] > 0.5, x, NEG_INF)
                xs.append(x)
                shifts.append(slopes[hf] * off)
        for hf in range(2):
            ids = [2 * b + hf for b in range(len(blocks))]
            m_new = m_old[hf]
            for i in ids:
                m_new = jnp.maximum(m_new, jnp.max(xs[i], axis=0, keepdims=True) + shifts[i])
            alpha = jnp.exp(m_old[hf] - m_new)
            l_new = alpha * l_old[hf]
            acc_new = alpha * acc_old[hf]
            for b, i in enumerate(ids):
                p = jnp.exp(xs[i] - (m_new - shifts[i]))
                l_new = l_new + jnp.sum(p, axis=0, keepdims=True)
                acc_new = acc_new + jnp.dot(vts[b], p.astype(BF16), preferred_element_type=F32)
            l_sc[hf] = l_new
            acc_sc[hf] = acc_new
            m_sc[hf] = m_new

    block_update([j], True)

    def body(i, carry):
        block_update([2 * i, 2 * i + 1], False)
        return carry

    lax.fori_loop(0, j // 2, body, 0)

    @pl.when(j % 2 == 1)
    def _():
        block_update([j - 1], False)

    o0 = acc_sc[0] / l_sc[0]
    o1 = acc_sc[1] / l_sc[1]
    if moba:
        first_head = lax.broadcasted_iota(jnp.int32, (LANE, tb), 0) < HALF
        out_ref[...] = jnp.where(first_head, o0, o1).T.astype(out_ref.dtype)
    else:
        lp = lam_ref[...]
        s1 = jnp.sum(lp[0:1] * lp[1:2], axis=1, keepdims=True)
        s2 = jnp.sum(lp[2:3] * lp[3:4], axis=1, keepdims=True)
        lam = jnp.exp(s1) - jnp.exp(s2) + lam_init
        o = (o0 - lam * o1).T
        o = o * lax.rsqrt(jnp.mean(o * o, axis=1, keepdims=True) + EPS) * g_ref[...]
        out_ref[...] = (o * (1.0 - lam_init)).astype(out_ref.dtype)


def _alibi_slopes(n_heads):
    return jnp.asarray([2.0 ** (-8.0 * (h + 1) / n_heads) for h in range(n_heads)], dtype=F32)


def _value_blocks_t(z, v_blk0, ngrp, bsz, seq):
    nb = seq // ATT_BLOCK
    v = z[:, v_blk0 * LANE:(v_blk0 + ngrp) * LANE].astype(BF16)
    return v.reshape(bsz, nb, ATT_BLOCK, ngrp, LANE).transpose(0, 3, 1, 4, 2)


def _attn_scratch(tb):
    return [pltpu.VMEM((2, 1, tb), F32), pltpu.VMEM((2, 1, tb), F32), pltpu.VMEM((2, LANE, tb), F32)]


def moba_attention(qn, kn, kmean, z, v_blk0, bsz, seq):
    t = bsz * seq
    tb = ATT_BLOCK
    nb = seq // tb
    ngrp = B_W // LANE
    km = jnp.pad(kmean.reshape(bsz, nb, B_W), ((0, 0), (0, LANE - nb), (0, 0)))
    return pl.pallas_call(
        functools.partial(_attn_kernel, moba=True, lam_init=None),
        grid=(bsz, ngrp, nb),
        in_specs=[pl.BlockSpec(memory_space=pltpu.SMEM),
                  pl.BlockSpec((tb, LANE), lambda b, g, j: (b * nb + j, g)),
                  pl.BlockSpec((seq, LANE), lambda b, g, j: (b, g)),
                  pl.BlockSpec((None, None, nb, LANE, tb), lambda b, g, j: (b, g, 0, 0, 0)),
                  pl.BlockSpec((None, LANE, LANE), lambda b, g, j: (b, 0, g))],
        out_specs=pl.BlockSpec((tb, LANE), lambda b, g, j: (b * nb + j, g)),
        out_shape=jax.ShapeDtypeStruct((t, B_W), F32),
        scratch_shapes=[pltpu.VMEM((2, LANE, tb), F32)] + _attn_scratch(tb),
        compiler_params=_cparams(("parallel", "parallel", "arbitrary"), VMEM_LIMIT),
        name="moba_attention",
    )(_alibi_slopes(B_HEADS), qn, kn, _value_blocks_t(z, v_blk0, ngrp, bsz, seq), km)


def diff_attention(qn, kn, z, v_blk0, lam_p, onorm_g, lam_init, bsz, seq):
    t = bsz * seq
    tb = ATT_BLOCK
    nb = seq // tb
    return pl.pallas_call(
        functools.partial(_attn_kernel, moba=False, lam_init=lam_init),
        grid=(bsz, C_HEADS, nb),
        in_specs=[pl.BlockSpec(memory_space=pltpu.SMEM),
                  pl.BlockSpec((tb, LANE), lambda b, g, j: (b * nb + j, g)),
                  pl.BlockSpec((seq, LANE), lambda b, g, j: (b, g)),
                  pl.BlockSpec((None, None, nb, LANE, tb), lambda b, g, j: (b, g, 0, 0, 0)),
                  pl.BlockSpec((4, C_DH), lambda b, g, j: (0, 0)),
                  pl.BlockSpec((1, C_VDH), lambda b, g, j: (0, 0))],
        out_specs=pl.BlockSpec((tb, LANE), lambda b, g, j: (b * nb + j, g)),
        out_shape=jax.ShapeDtypeStruct((t, C_HEADS * C_VDH), F32),
        scratch_shapes=_attn_scratch(tb),
        compiler_params=_cparams(("parallel", "parallel", "arbitrary"), VMEM_LIMIT),
        name="diff_attention",
    )(_alibi_slopes(C_HEADS), qn, kn, _value_blocks_t(z, v_blk0, C_HEADS, bsz, seq), lam_p,
      onorm_g.reshape(1, C_VDH))


def _top_rows(s, count):
    n = s.shape[0]
    rows = lax.broadcasted_iota(jnp.int32, s.shape, 0)
    vals, idxs = [], []
    for _ in range(count):
        mx = jnp.max(s, axis=0, keepdims=True)
        ix = jnp.min(jnp.where(s == mx, rows, n), axis=0, keepdims=True)
        vals.append(mx)
        idxs.append(ix)
        s = jnp.where(rows == ix, NEG_INF, s)
    return jnp.concatenate(vals, axis=0), jnp.concatenate(idxs, axis=0)


def _peer_route_kernel(q_ref, keys_ref, eid_ref, gate_ref):
    kk = PEER_TOPK
    half = PEER_DK // 2
    tops = []
    for p in range(2):
        qp = q_ref[:, p * half:(p + 1) * half]
        st = lax.dot_general(keys_ref[p], qp, (((1,), (1,)), ((), ())), precision=HIGHEST,
                             preferred_element_type=F32)
        tops.append(_top_rows(st, kk))
    (s0, i0), (s1, i1) = tops
    widths = [kk // (a + 1) for a in range(kk)]
    pad = -sum(widths) % 8
    cand_s = jnp.concatenate([s0[a:a + 1] + s1[0:widths[a]] for a in range(kk)]
                             + [jnp.full((pad, s0.shape[1]), NEG_INF, F32)], axis=0)
    cand_i = jnp.concatenate([i0[a:a + 1] * PEER_NKEYS + i1[0:widths[a]] for a in range(kk)]
                             + [jnp.zeros((pad, s0.shape[1]), jnp.int32)], axis=0)
    score, pos = _top_rows(cand_s, kk)
    rows = lax.broadcasted_iota(jnp.int32, cand_i.shape, 0)
    eid = jnp.concatenate(
        [jnp.sum(jnp.where(rows == pos[a:a + 1], cand_i, 0), axis=0, keepdims=True) for a in range(kk)], axis=0)
    e = jnp.exp(score - score[0:1])
    eid_ref[...] = eid
    gate_ref[...] = e / jnp.sum(e, axis=0, keepdims=True)


def peer_route(qry, keys, *, tt=256):
    t = qry.shape[0]
    hk = PEER_HEADS * PEER_TOPK
    return pl.pallas_call(
        _peer_route_kernel,
        grid=(t // tt, PEER_HEADS),
        in_specs=[pl.BlockSpec((tt, PEER_DK), lambda i, h: (i, h)),
                  pl.BlockSpec((2, PEER_NKEYS, PEER_DK // 2), lambda i, h: (0, 0, 0))],
        out_specs=[pl.BlockSpec((PEER_TOPK, tt), lambda i, h: (h, i)),
                   pl.BlockSpec((PEER_TOPK, tt), lambda i, h: (h, i))],
        out_shape=[jax.ShapeDtypeStruct((hk, t), jnp.int32), jax.ShapeDtypeStruct((hk, t), F32)],
        compiler_params=_cparams(("parallel", "parallel")),
        name="peer_route",
    )(qry, keys)


ROW_SUB = 4
PEER_TT = 64


PEER_HK = PEER_HEADS * PEER_TOPK
ROW_BF = 2 * ROW_SUB
GROUP = 8


def pack_table(tab):
    n, d = tab.shape
    b = lax.bitcast_convert_type(tab.astype(BF16), jnp.uint16).astype(jnp.uint32)
    b = b.reshape(n, ROW_SUB, 2, LANE)
    return b[:, :, 0, :] | (b[:, :, 1, :] << 16)


def _split_bf16(x):
    hi = x.astype(BF16)
    lo = (x - hi.astype(F32)).astype(BF16)
    return jnp.concatenate([hi, lo], axis=0)


def _gelu(x):
    return 0.5 * x * (1.0 + lax.erf(x * (2.0 ** -0.5)))


def _gather_rows(idx_refs, tab_ref, dst_ref, t):
    for k in range(PEER_HK):
        dst_ref[k * ROW_SUB:(k + 1) * ROW_SUB, :] = tab_ref[idx_refs[k // PEER_TOPK][t, k % PEER_TOPK]]


def _diag_mask(rows):
    shape = (rows, PEER_HK * ROW_BF)
    return (lax.broadcasted_iota(jnp.int32, shape, 1) % ROW_BF) == (lax.broadcasted_iota(jnp.int32, shape, 0) % ROW_BF)


def _peer_up_kernel(*refs):
    idx_refs = refs[:PEER_HEADS]
    h_ref, gate_ref, tab_ref, act_ref, g0_sc, g1_sc, c_sc = refs[PEER_HEADS:]
    slots = (g0_sc, g1_sc)
    wide = PEER_HK * ROW_BF
    mask = _diag_mask(2 * ROW_BF)
    fold = (lax.broadcasted_iota(jnp.int32, (wide, PEER_HK), 0) // ROW_BF
            == lax.broadcasted_iota(jnp.int32, (wide, PEER_HK), 1)).astype(BF16)

    def group(g, carry):
        t0 = g * GROUP
        _gather_rows(idx_refs,tab_ref, slots[0], t0)
        for i in range(GROUP):
            if i + 1 < GROUP:
                _gather_rows(idx_refs,tab_ref, slots[(i + 1) % 2], t0 + i + 1)
            rows = pltpu.bitcast(slots[i % 2][...], BF16)
            xs = _split_bf16(h_ref[t0 + i])
            y = lax.dot_general(xs, rows, (((1,), (1,)), ((), ())), preferred_element_type=F32)
            y = jnp.where(mask, y, 0.0)
            c_sc[i:i + 1, :] = jnp.sum(y, axis=0, keepdims=True)
        s = jnp.dot(_split_bf16(c_sc[...]), fold, preferred_element_type=F32)
        s = s[0:GROUP] + s[GROUP:]
        rows8 = pl.ds(pl.multiple_of(t0, GROUP), GROUP)
        act_ref[rows8, :] = _gelu(s) * gate_ref[rows8, :]
        return carry

    lax.fori_loop(0, PEER_TT // GROUP, group, 0)


def _peer_down_kernel(*refs):
    idx_refs = refs[:PEER_HEADS]
    act_ref, x_ref, g_ref, tab_ref, out_ref, g0_sc, g1_sc = refs[PEER_HEADS:]
    slots = (g0_sc, g1_sc)
    wide = PEER_HK * ROW_BF
    mask = _diag_mask(ROW_BF)
    spread = (lax.broadcasted_iota(jnp.int32, (PEER_HK, wide), 1) // ROW_BF
              == lax.broadcasted_iota(jnp.int32, (PEER_HK, wide), 0)).astype(BF16)
    gate = g_ref[...]

    def group(g, carry):
        t0 = g * GROUP
        _gather_rows(idx_refs,tab_ref, slots[0], t0)
        acts = act_ref[pl.ds(pl.multiple_of(t0, GROUP), GROUP), :]
        a_wide = jnp.dot(_split_bf16(acts), spread, preferred_element_type=F32)
        for i in range(GROUP):
            if i + 1 < GROUP:
                _gather_rows(idx_refs,tab_ref, slots[(i + 1) % 2], t0 + i + 1)
            rows = pltpu.bitcast(slots[i % 2][...], BF16)
            a_hi = jnp.where(mask, jnp.broadcast_to(a_wide[i:i + 1], (ROW_BF, wide)), 0.0)
            a_lo = jnp.where(mask, jnp.broadcast_to(a_wide[GROUP + i:GROUP + i + 1], (ROW_BF, wide)), 0.0)
            lhs = jnp.concatenate([a_hi, a_lo], axis=0).astype(BF16)
            y = jnp.dot(lhs, rows, preferred_element_type=F32)
            out_ref[t0 + i] = x_ref[t0 + i] + gate * (y[0:ROW_BF] + y[ROW_BF:])
        return carry

    lax.fori_loop(0, PEER_TT // GROUP, group, 0)


def _table_spec(n):
    return pl.BlockSpec((n, ROW_SUB, LANE), lambda i: (0, 0, 0), pipeline_mode=pl.Buffered(1))


def _idx_specs(tt):
    return [pl.BlockSpec((tt, PEER_TOPK), lambda i: (i, 0), memory_space=pltpu.SMEM) for _ in range(PEER_HEADS)]


def peer_up(idxs, h, gate, tab_u):
    t, hk = gate.shape
    n = tab_u.shape[0]
    tt = PEER_TT
    return pl.pallas_call(
        _peer_up_kernel,
        grid=(t // tt,),
        in_specs=_idx_specs(tt) + [
                  pl.BlockSpec((tt, 8, LANE), lambda i: (i, 0, 0)),
                  pl.BlockSpec((tt, hk), lambda i: (i, 0)),
                  _table_spec(n)],
        out_specs=pl.BlockSpec((tt, hk), lambda i: (i, 0)),
        out_shape=jax.ShapeDtypeStruct((t, hk), F32),
        scratch_shapes=[pltpu.VMEM((hk * ROW_SUB, LANE), jnp.uint32), pltpu.VMEM((hk * ROW_SUB, LANE), jnp.uint32),
                        pltpu.VMEM((GROUP, hk * ROW_BF), F32)],
        compiler_params=_cparams(("arbitrary",), VMEM_LIMIT),
        name="peer_up",
    )(*idxs, h.reshape(t, 8, LANE), gate, tab_u)


def peer_down(idxs, act, x, gate2, tab_v, seq):
    t, hk = act.shape
    n = tab_v.shape[0]
    d = x.shape[1]
    tt = PEER_TT
    tiles_per_batch = seq // tt
    out = pl.pallas_call(
        _peer_down_kernel,
        grid=(t // tt,),
        in_specs=_idx_specs(tt) + [
                  pl.BlockSpec((tt, hk), lambda i: (i, 0)),
                  pl.BlockSpec((tt, 8, LANE), lambda i: (i, 0, 0)),
                  pl.BlockSpec((None, 8, LANE), lambda i: (i // tiles_per_batch, 0, 0)),
                  _table_spec(n)],
        out_specs=pl.BlockSpec((tt, 8, LANE), lambda i: (i, 0, 0)),
        out_shape=jax.ShapeDtypeStruct((t, 8, LANE), F32),
        scratch_shapes=[pltpu.VMEM((hk * ROW_SUB, LANE), jnp.uint32), pltpu.VMEM((hk * ROW_SUB, LANE), jnp.uint32)],
        compiler_params=_cparams(("arbitrary",), VMEM_LIMIT),
        name="peer_down",
    )(*idxs, act, x.reshape(t, 8, LANE), gate2.reshape(-1, 8, LANE), tab_v)
    return out.reshape(t, d)


def peer_ffn(x, g, sc, sh, gate2, wq, keys, tab_u, tab_v, seq):
    qry, h = norm_linear(x, g, sc, sh, wq, jnp.zeros((wq.shape[1],), F32), seq, emit_h=True)
    eid_t, gate_t = peer_route(qry, keys)
    idxs = [eid_t[hd * PEER_TOPK:(hd + 1) * PEER_TOPK].T for hd in range(PEER_HEADS)]
    act = peer_up(idxs, h, gate_t.T, tab_u)
    return peer_down(idxs, act, x, gate2, tab_v, seq)


def _even_mixer(xt, g, sc, sh, gate, w_in, conv_w, igate_b, fgate_b, mnorm_g, qn_g, kn_g, w_out, bsz, seq):
    d = xt.shape[1]
    n_gate = 2 * M_HEADS
    g0 = 4 * M_W
    w_main = jnp.concatenate([w_in[:, :g0], w_in[:, g0 + n_gate:]], axis=1).astype(BF16)
    w_gate = jnp.pad(w_in[:, g0:g0 + n_gate], ((0, 0), (0, LANE - n_gate)))
    b_gate = jnp.pad(jnp.concatenate([igate_b, fgate_b]), (0, LANE - n_gate))
    z = norm_linear(xt, g, sc, sh, w_main, jnp.zeros((w_main.shape[1],), F32), seq)
    zg = norm_linear(xt, g, sc, sh, w_gate, b_gate, seq)
    icol, fcol, irow, frow = split_gates(zg)
    hm = mlstm_mixer(z, conv_w, icol, fcol, irow, frow, mnorm_g, bsz, seq)
    blk = g0 // LANE
    nblk = B_W // LANE
    qn, kn, kmean = head_norm(z, blk, blk + nblk, nblk, qn_g, kn_g, True)
    hb = moba_attention(qn, kn, kmean, z, blk + 2 * nblk, bsz, seq)
    w_out = w_out.astype(BF16)
    return linear_residual([hm, hb], [w_out[:M_W], w_out[M_W:]], xt, gate, seq)


def _odd_mixer(xt, g, sc, sh, gate, w_in, qn_g, kn_g, lam_p, onorm_g, w_out, lam_init, bsz, seq):
    z = norm_linear(xt, g, sc, sh, w_in.astype(BF16), jnp.zeros((w_in.shape[1],), F32), seq)
    nblk = 2 * C_HEADS * C_DH // LANE
    qn, kn = head_norm(z, 0, nblk, nblk, qn_g, kn_g, False)
    o = diff_attention(qn, kn, z, 2 * nblk, lam_p, onorm_g, lam_init, bsz, seq)
    return linear_residual([o], [w_out.astype(BF16)], xt, gate, seq)


def kernel(x, c, ada_w, ada_b, norm_mix_g, norm_ffn_g, ev_w_in, ev_conv_w, ev_igate_b, ev_fgate_b, ev_mnorm_g,
           ev_qn_g, ev_kn_g, ev_w_out, od_w_in, od_qn_g, od_kn_g, od_lam, od_onorm_g, od_w_out, peer_wq,
           peer_keys, peer_u, peer_v):
    bsz, seq, d = x.shape
    depth = ada_w.shape[0]
    mod = adaln_mod(c, ada_w, ada_b).reshape(depth, bsz, 6, 1, d)
    xt = x.reshape(bsz * seq, d)
    for layer in range(depth):
        sh1, sc1, g1, sh2, sc2, g2 = [mod[layer, :, i] for i in range(6)]
        if layer % 2 == 0:
            e = layer // 2
            xt = _even_mixer(xt, norm_mix_g[layer], sc1, sh1, g1, ev_w_in[e], ev_conv_w[e], ev_igate_b[e],
                             ev_fgate_b[e], ev_mnorm_g[e], ev_qn_g[e], ev_kn_g[e], ev_w_out[e], bsz, seq)
        else:
            o = layer // 2
            lam_init = 0.8 - 0.6 * math.exp(-0.3 * layer)
            xt = _odd_mixer(xt, norm_mix_g[layer], sc1, sh1, g1, od_w_in[o], od_qn_g[o], od_kn_g[o], od_lam[o],
                            od_onorm_g[o], od_w_out[o], lam_init, bsz, seq)
        xt = peer_ffn(xt, norm_ffn_g[layer], sc2, sh2, g2, peer_wq[layer].astype(BF16), peer_keys[layer],
                      pack_table(peer_u[layer]), pack_table(peer_v[layer]), seq)
    return xt.reshape(bsz, seq, d)
```

```python
import functools
import math

import jax
import jax.numpy as jnp
from jax import lax
from jax.experimental import pallas as pl
from jax.experimental.pallas import tpu as pltpu

F32 = jnp.float32
BF16 = jnp.bfloat16
HIGHEST = lax.Precision.HIGHEST
NEG_INF = float("-inf")

EPS = 1e-6
D_MODEL = 1024
M_HEADS, M_DH, M_CHUNK, CONV_K = 4, 128, 128, 4
M_W = M_HEADS * M_DH
B_HEADS, B_DH, MOBA_BLOCK, MOBA_TOPK = 8, 64, 256, 3
B_W = B_HEADS * B_DH
C_HEADS, C_DH, C_VDH = 8, 64, 128
PEER_HEADS, PEER_NKEYS, PEER_DK, PEER_TOPK = 8, 128, 256, 16

LANE = 128
VMEM_LIMIT = 56 * 1024 * 1024


def _cparams(sem, vmem=None):
    return pltpu.CompilerParams(dimension_semantics=sem, vmem_limit_bytes=vmem)


def _mod_kernel(c_ref, w_ref, b_ref, o_ref):
    c = c_ref[...]
    ca = c * (1.0 / (1.0 + jnp.exp(-c)))
    o_ref[...] = jnp.dot(ca, w_ref[...], precision=HIGHEST, preferred_element_type=F32) + b_ref[...]


def adaln_mod(c, ada_w, ada_b):
    depth, d, n = ada_w.shape
    bsz = c.shape[0]
    tn = 512
    return pl.pallas_call(
        _mod_kernel,
        grid=(depth, n // tn),
        in_specs=[pl.BlockSpec((bsz, d), lambda l, j: (0, 0)),
                  pl.BlockSpec((None, d, tn), lambda l, j: (l, 0, j)),
                  pl.BlockSpec((None, 1, tn), lambda l, j: (l, 0, j))],
        out_specs=pl.BlockSpec((None, bsz, tn), lambda l, j: (l, 0, j)),
        out_shape=jax.ShapeDtypeStruct((depth, bsz, n), F32),
        compiler_params=_cparams(("parallel", "parallel")),
        name="adaln_mod",
    )(c, ada_w, ada_b.reshape(depth, 1, n))


def _norm_linear_kernel(x_ref, g_ref, sc_ref, sh_ref, w_ref, b_ref, o_ref, *rest, emit_h, precise):
    if emit_h:
        h_out_ref, h_sc = rest
    else:
        (h_sc,) = rest

    @pl.when(pl.program_id(1) == 0)
    def _():
        x = x_ref[...]
        y = x * lax.rsqrt(jnp.mean(x * x, axis=-1, keepdims=True) + EPS) * g_ref[...]
        h = y * (1.0 + sc_ref[...]) + sh_ref[...]
        h_sc[...] = h.astype(h_sc.dtype)
        if emit_h:
            h_out_ref[...] = h

    if precise:
        acc = jnp.dot(h_sc[...], w_ref[...], precision=HIGHEST, preferred_element_type=F32)
    else:
        acc = jnp.dot(h_sc[...], w_ref[...], preferred_element_type=F32)
    o_ref[...] = (acc + b_ref[...]).astype(o_ref.dtype)


def norm_linear(x, g, sc, sh, w, bias, seq, *, tm=512, tn=512, emit_h=False, out_dtype=F32):
    t, d = x.shape
    n = w.shape[1]
    tn = min(tn, n)
    precise = w.dtype == F32
    rows_per_batch = seq // tm
    out_shape = [jax.ShapeDtypeStruct((t, n), out_dtype)]
    out_specs = [pl.BlockSpec((tm, tn), lambda i, j: (i, j))]
    if emit_h:
        out_shape.append(jax.ShapeDtypeStruct((t, d), F32))
        out_specs.append(pl.BlockSpec((tm, d), lambda i, j: (i, 0)))
    res = pl.pallas_call(
        functools.partial(_norm_linear_kernel, emit_h=emit_h, precise=precise),
        grid=(t // tm, n // tn),
        in_specs=[pl.BlockSpec((tm, d), lambda i, j: (i, 0)),
                  pl.BlockSpec((1, d), lambda i, j: (0, 0)),
                  pl.BlockSpec((None, 1, d), lambda i, j: (i // rows_per_batch, 0, 0)),
                  pl.BlockSpec((None, 1, d), lambda i, j: (i // rows_per_batch, 0, 0)),
                  pl.BlockSpec((d, tn), lambda i, j: (0, j)),
                  pl.BlockSpec((1, tn), lambda i, j: (0, j))],
        out_specs=out_specs,
        out_shape=out_shape,
        scratch_shapes=[pltpu.VMEM((tm, d), w.dtype)],
        compiler_params=_cparams(("parallel", "arbitrary"), VMEM_LIMIT),
        name="norm_linear",
    )(x, g.reshape(1, d), sc, sh, w, bias.reshape(1, n))
    return res if emit_h else res[0]


def _linear_residual_kernel(*refs, nparts):
    y_refs, w_refs = refs[:nparts], refs[nparts:2 * nparts]
    x_ref, gate_ref, o_ref = refs[2 * nparts:]
    acc = None
    for y_ref, w_ref in zip(y_refs, w_refs):
        part = jnp.dot(y_ref[...].astype(BF16), w_ref[...], preferred_element_type=F32)
        acc = part if acc is None else acc + part
    o_ref[...] = x_ref[...] + gate_ref[...] * acc


def linear_residual(ys, ws, x, gate, seq, *, tm=512):
    t, d = x.shape
    rows_per_batch = seq // tm
    return pl.pallas_call(
        functools.partial(_linear_residual_kernel, nparts=len(ys)),
        grid=(t // tm,),
        in_specs=([pl.BlockSpec((tm, y.shape[1]), lambda i: (i, 0)) for y in ys]
                  + [pl.BlockSpec(w.shape, lambda i: (0, 0)) for w in ws]
                  + [pl.BlockSpec((tm, d), lambda i: (i, 0)),
                     pl.BlockSpec((None, 1, d), lambda i: (i // rows_per_batch, 0, 0))]),
        out_specs=pl.BlockSpec((tm, d), lambda i: (i, 0)),
        out_shape=jax.ShapeDtypeStruct((t, d), F32),
        compiler_params=_cparams(("parallel",), VMEM_LIMIT),
        name="linear_residual",
    )(*ys, *ws, x, gate)


def _shift_rows(x, tail, s):
    if s == 0:
        return x
    xs = pltpu.roll(x, s, axis=0)
    ts = pltpu.roll(tail, s, axis=0)
    row = lax.broadcasted_iota(jnp.int32, (8, x.shape[1]), 0)
    top = jnp.where(row < s, ts, xs[0:8])
    return jnp.concatenate([top, xs[8:]], axis=0)


def _conv_silu(x, tail, w):
    acc = None
    for j in range(CONV_K):
        term = w[j:j + 1, :] * _shift_rows(x, tail, CONV_K - 1 - j)
        acc = term if acc is None else acc + term
    return acc * (1.0 / (1.0 + jnp.exp(-acc)))


def _log_sigmoid(x):
    return jnp.minimum(x, 0.0) - jnp.log(1.0 + jnp.exp(-jnp.abs(x)))


def _mlstm_kernel(q_ref, k_ref, v_ref, o_ref, wq_ref, wk_ref, icol_ref, fcol_ref, irow_ref, frow_ref,
                  g_ref, out_ref, c_sc, n_sc, m_sc, qt_sc, kt_sc):
    L = M_CHUNK

    @pl.when(pl.program_id(2) == 0)
    def _():
        c_sc[...] = jnp.zeros_like(c_sc)
        n_sc[...] = jnp.zeros_like(n_sc)
        m_sc[...] = jnp.zeros_like(m_sc)
        qt_sc[...] = jnp.zeros_like(qt_sc)
        kt_sc[...] = jnp.zeros_like(kt_sc)

    q_raw = q_ref[...]
    k_raw = k_ref[...]
    q = _conv_silu(q_raw, qt_sc[...], wq_ref[...])
    k = _conv_silu(k_raw, kt_sc[...], wk_ref[...]) * (M_DH ** -0.5)
    qt_sc[...] = q_raw[L - 8:, :]
    kt_sc[...] = k_raw[L - 8:, :]
    v = v_ref[...]

    row = lax.broadcasted_iota(jnp.int32, (L, L), 0)
    col = lax.broadcasted_iota(jnp.int32, (L, L), 1)
    causal = row >= col
    tri = jnp.where(causal, 1.0, 0.0).astype(F32)
    triu = jnp.where(row <= col, 1.0, 0.0).astype(F32)

    lf_col = _log_sigmoid(fcol_ref[...])
    lf_row = _log_sigmoid(frow_ref[...])
    ig_col = icol_ref[...]
    ig_row = irow_ref[...]
    b_col = jnp.dot(tri, jnp.broadcast_to(lf_col, (L, L)), precision=HIGHEST,
                    preferred_element_type=F32)[:, 0:1]
    b_row = jnp.dot(jnp.broadcast_to(lf_row, (8, L)), triu, precision=HIGHEST,
                    preferred_element_type=F32)[0:1, :]
    b_end = b_row[:, L - 1:L]

    m_prev = m_sc[...]
    c_prev = c_sc[...]
    n_prev = n_sc[...]

    log_d = jnp.where(causal, b_col - b_row + ig_row, NEG_INF)
    a_inter = b_col + m_prev
    m_t = jnp.maximum(a_inter, jnp.max(log_d, axis=1, keepdims=True))
    w_inter = jnp.exp(a_inter - m_t)
    qb = q.astype(BF16)
    kb = k.astype(BF16)
    vb = v.astype(BF16)
    s = lax.dot_general(qb, kb, (((1,), (1,)), ((), ())), preferred_element_type=F32)
    s = s * jnp.exp(log_d - m_t)
    num = (w_inter * jnp.dot(qb, c_prev.astype(BF16), preferred_element_type=F32)
           + jnp.dot(s.astype(BF16), vb, preferred_element_type=F32))
    den = (w_inter * jnp.sum(q * n_prev, axis=1, keepdims=True) + jnp.sum(s, axis=1, keepdims=True))
    h = num / jnp.maximum(jnp.abs(den), jnp.exp(-m_t))

    w_end_row = b_end - b_row + ig_row
    m_loc = jnp.max(w_end_row, axis=1, keepdims=True)
    e_col = jnp.exp(b_end - b_col + ig_col - m_loc)
    ke = k * e_col
    c_loc = lax.dot_general(ke.astype(BF16), vb, (((0,), (0,)), ((), ())), preferred_element_type=F32)
    n_loc = jnp.sum(ke, axis=0, keepdims=True)
    m_new = jnp.maximum(b_end + m_prev, m_loc)
    a = jnp.exp(b_end + m_prev - m_new)
    sc = jnp.exp(m_loc - m_new)
    c_sc[...] = a * c_prev + sc * c_loc
    n_sc[...] = a * n_prev + sc * n_loc
    m_sc[...] = m_new

    hn = h * lax.rsqrt(jnp.mean(h * h, axis=1, keepdims=True) + EPS) * g_ref[...]
    og = o_ref[...]
    out_ref[...] = (hn * (1.0 / (1.0 + jnp.exp(-og)))).astype(out_ref.dtype)


def mlstm_mixer(z, conv_w, icol, fcol, irow, frow, mnorm_g, bsz, seq):
    t = bsz * seq
    L = M_CHUNK
    nc = seq // L
    hq = M_W // LANE

    def zspec(sec):
        return pl.BlockSpec((L, M_DH), lambda b, h, c, sec=sec: (b * nc + c, sec * hq + h))

    return pl.pallas_call(
        _mlstm_kernel,
        grid=(bsz, M_HEADS, nc),
        in_specs=[zspec(0), zspec(1), zspec(2), zspec(3),
                  pl.BlockSpec((CONV_K, M_DH), lambda b, h, c: (0, h)),
                  pl.BlockSpec((CONV_K, M_DH), lambda b, h, c: (0, hq + h)),
                  pl.BlockSpec((None, L, 1), lambda b, h, c: (h, b * nc + c, 0)),
                  pl.BlockSpec((None, L, 1), lambda b, h, c: (h, b * nc + c, 0)),
                  pl.BlockSpec((None, None, 1, L), lambda b, h, c: (h, b * nc + c, 0, 0)),
                  pl.BlockSpec((None, None, 1, L), lambda b, h, c: (h, b * nc + c, 0, 0)),
                  pl.BlockSpec((None, 1, M_DH), lambda b, h, c: (h, 0, 0))],
        out_specs=pl.BlockSpec((L, M_DH), lambda b, h, c: (b * nc + c, h)),
        out_shape=jax.ShapeDtypeStruct((t, M_W), F32),
        scratch_shapes=[pltpu.VMEM((M_DH, M_DH), F32), pltpu.VMEM((1, M_DH), F32), pltpu.VMEM((1, 1), F32),
                        pltpu.VMEM((8, M_DH), F32), pltpu.VMEM((8, M_DH), F32)],
        compiler_params=_cparams(("parallel", "parallel", "arbitrary")),
        name="mlstm",
    )(z, z, z, z, conv_w, conv_w, icol, fcol, irow, frow, mnorm_g.reshape(M_HEADS, 1, M_DH))


def split_gates(zg):
    t = zg.shape[0]
    gt = zg[:, :2 * M_HEADS].T
    cols = gt.reshape(2 * M_HEADS, t, 1)
    rows = gt.reshape(2 * M_HEADS, t // M_CHUNK, 1, M_CHUNK)
    return cols[:M_HEADS], cols[M_HEADS:], rows[:M_HEADS], rows[M_HEADS:]


ATT_BLOCK = 256
ATT_UNROLL = 4
HALF = 64


def _half_rms_norm(x, g):
    lo = lax.broadcasted_iota(jnp.int32, x.shape, 1) < HALF
    x2 = x * x
    s_lo = jnp.sum(jnp.where(lo, x2, 0.0), axis=1, keepdims=True)
    s_hi = jnp.sum(jnp.where(lo, 0.0, x2), axis=1, keepdims=True)
    r = jnp.where(lo, lax.rsqrt(s_lo * (1.0 / HALF) + EPS), lax.rsqrt(s_hi * (1.0 / HALF) + EPS))
    return x * r * g


def _head_norm_kernel(q_ref, k_ref, gq_ref, gk_ref, qn_ref, kn_ref, *rest):
    for c in range(q_ref.shape[1] // LANE):
        cols = slice(c * LANE, (c + 1) * LANE)
        qn_ref[:, cols] = _half_rms_norm(q_ref[:, cols], gq_ref[...])
        kn = _half_rms_norm(k_ref[:, cols], gk_ref[...])
        kn_ref[:, cols] = kn.astype(kn_ref.dtype)
        if rest:
            rest[0][:, cols] = jnp.mean(kn, axis=0, keepdims=True)


def head_norm(z, q_blk0, k_blk0, nblk, gq, gk, with_kmean):
    t = z.shape[0]
    w = nblk * LANE
    tb = ATT_BLOCK
    gq2 = jnp.concatenate([gq, gq]).reshape(1, LANE)
    gk2 = jnp.concatenate([gk, gk]).reshape(1, LANE)
    assert q_blk0 % nblk == 0 and k_blk0 % nblk == 0
    out_shape = [jax.ShapeDtypeStruct((t, w), F32), jax.ShapeDtypeStruct((t, w), BF16)]
    out_specs = [pl.BlockSpec((tb, w), lambda i: (i, 0)), pl.BlockSpec((tb, w), lambda i: (i, 0))]
    if with_kmean:
        out_shape.append(jax.ShapeDtypeStruct((t // tb, 1, w), F32))
        out_specs.append(pl.BlockSpec((None, 1, w), lambda i: (i, 0, 0)))
    return pl.pallas_call(
        _head_norm_kernel,
        grid=(t // tb,),
        in_specs=[pl.BlockSpec((tb, w), lambda i: (i, q_blk0 // nblk)),
                  pl.BlockSpec((tb, w), lambda i: (i, k_blk0 // nblk)),
                  pl.BlockSpec((1, LANE), lambda i: (0, 0)),
                  pl.BlockSpec((1, LANE), lambda i: (0, 0))],
        out_specs=out_specs,
        out_shape=out_shape,
        compiler_params=_cparams(("parallel",)),
        name="head_norm",
    )(z, z, gq2, gk2)


def _attn_kernel(*refs, moba, lam_init):
    if moba:
        (slope_ref, q_ref, k_ref, vt_ref, km_ref, out_ref, sel_sc, m_sc, l_sc, acc_sc) = refs
    else:
        (slope_ref, q_ref, k_ref, vt_ref, lam_ref, g_ref, out_ref, m_sc, l_sc, acc_sc) = refs
    tb = ATT_BLOCK
    grp = pl.program_id(1)
    j = pl.program_id(2)
    scale = HALF ** -0.5

    q = q_ref[...]
    lo = lax.broadcasted_iota(jnp.int32, (tb, LANE), 1) < HALF
    qs = q * scale
    q_half = [jnp.where(lo, qs, 0.0).astype(BF16), jnp.where(lo, 0.0, qs).astype(BF16)]
    krow = lax.broadcasted_iota(jnp.int32, (tb, tb), 0)
    causal = krow <= lax.broadcasted_iota(jnp.int32, (tb, tb), 1)
    if moba:
        slopes = [slope_ref[2 * grp], slope_ref[2 * grp + 1]]
    else:
        slopes = [slope_ref[grp], slope_ref[grp]]
    kpos = krow.astype(F32)
    bias = [slopes[0] * kpos, slopes[1] * kpos] if moba else [slopes[0] * kpos] * 2

    if moba:
        km = km_ref[...]
        blk = lax.broadcasted_iota(jnp.int32, (LANE, tb), 0)
        for hf in range(2):
            qm = jnp.where(lo, q, 0.0) if hf == 0 else jnp.where(lo, 0.0, q)
            gate = lax.dot_general(km, qm, (((1,), (1,)), ((), ())), precision=HIGHEST,
                                   preferred_element_type=F32)
            gate = jnp.where(blk < j, gate, NEG_INF)
            sel = jnp.zeros((LANE, tb), F32)
            for _ in range(MOBA_TOPK):
                mx = jnp.max(gate, axis=0, keepdims=True)
                idx = jnp.min(jnp.where(gate == mx, blk, LANE), axis=0, keepdims=True)
                hit = blk == idx
                sel = jnp.where(jnp.logical_and(hit, mx > NEG_INF), 1.0, sel)
                gate = jnp.where(hit, NEG_INF, gate)
            sel_sc[hf] = sel

    m_sc[...] = jnp.full(m_sc.shape, NEG_INF, F32)
    l_sc[...] = jnp.zeros(l_sc.shape, F32)
    acc_sc[...] = jnp.zeros(acc_sc.shape, F32)

    def block_update(blocks, diag):
        m_old = [m_sc[0], m_sc[1]]
        l_old = [l_sc[0], l_sc[1]]
        acc_old = [acc_sc[0], acc_sc[1]]
        xs, shifts, vts = [], [], []
        for n in blocks:
            start = pl.multiple_of(n * tb, tb)
            kb = k_ref[pl.ds(start, tb), :]
            vts.append(vt_ref[n])
            off = ((n - j) * tb).astype(F32)
            for hf in range(2):
                x = lax.dot_general(kb, q_half[hf], (((1,), (1,)), ((), ())), preferred_element_type=F32)
                x = x + bias[hf]
                if diag:
                    x = jnp.where(causal, x, NEG_INF)
                elif moba:
                    x = jnp.where(sel_sc[hf, pl.ds(n, 1), :] > 0.5, x, NEG_INF)
                xs.append(x)
                shifts.append(slopes[hf] * off)
        for hf in range(2):
            ids = [2 * b + hf for b in range(len(blocks))]
            m_new = m_old[hf]
            for i in ids:
                m_new = jnp.maximum(m_new, jnp.max(xs[i], axis=0, keepdims=True) + shifts[i])
            alpha = jnp.exp(m_old[hf] - m_new)
            l_new = alpha * l_old[hf]
            acc_new = alpha * acc_old[hf]
            for b, i in enumerate(ids):
                p = jnp.exp(xs[i] - (m_new - shifts[i]))
                l_new = l_new + jnp.sum(p, axis=0, keepdims=True)
                acc_new = acc_new + jnp.dot(vts[b], p.astype(BF16), preferred_element_type=F32)
            l_sc[hf] = l_new
            acc_sc[hf] = acc_new
            m_sc[hf] = m_new

    block_update([j], True)

    def body(i, carry):
        block_update([ATT_UNROLL * i + u for u in range(ATT_UNROLL)], False)
        return carry

    lax.fori_loop(0, j // ATT_UNROLL, body, 0)

    done = (j // ATT_UNROLL) * ATT_UNROLL
    width = ATT_UNROLL // 2
    while width >= 1:
        has = (j & width) != 0

        @pl.when(has)
        def _(done=done, width=width):
            block_update([done + u for u in range(width)], False)

        done = done + jnp.where(has, width, 0)
        width //= 2

    o0 = acc_sc[0] / l_sc[0]
    o1 = acc_sc[1] / l_sc[1]
    if moba:
        first_head = lax.broadcasted_iota(jnp.int32, (LANE, tb), 0) < HALF
        out_ref[...] = jnp.where(first_head, o0, o1).T.astype(out_ref.dtype)
    else:
        lp = lam_ref[...]
        s1 = jnp.sum(lp[0:1] * lp[1:2], axis=1, keepdims=True)
        s2 = jnp.sum(lp[2:3] * lp[3:4], axis=1, keepdims=True)
        lam = jnp.exp(s1) - jnp.exp(s2) + lam_init
        o = (o0 - lam * o1).T
        o = o * lax.rsqrt(jnp.mean(o * o, axis=1, keepdims=True) + EPS) * g_ref[...]
        out_ref[...] = (o * (1.0 - lam_init)).astype(out_ref.dtype)


def _alibi_slopes(n_heads):
    return jnp.asarray([2.0 ** (-8.0 * (h + 1) / n_heads) for h in range(n_heads)], dtype=F32)


def _value_blocks_t(z, v_blk0, ngrp, bsz, seq):
    nb = seq // ATT_BLOCK
    v = z[:, v_blk0 * LANE:(v_blk0 + ngrp) * LANE].astype(BF16)
    return v.reshape(bsz, nb, ATT_BLOCK, ngrp, LANE).transpose(0, 3, 1, 4, 2)


def _attn_scratch(tb):
    return [pltpu.VMEM((2, 1, tb), F32), pltpu.VMEM((2, 1, tb), F32), pltpu.VMEM((2, LANE, tb), F32)]


def moba_attention(qn, kn, kmean, z, v_blk0, bsz, seq):
    t = bsz * seq
    tb = ATT_BLOCK
    nb = seq // tb
    ngrp = B_W // LANE
    km = jnp.pad(kmean.reshape(bsz, nb, B_W), ((0, 0), (0, LANE - nb), (0, 0)))
    return pl.pallas_call(
        functools.partial(_attn_kernel, moba=True, lam_init=None),
        grid=(bsz, ngrp, nb),
        in_specs=[pl.BlockSpec(memory_space=pltpu.SMEM),
                  pl.BlockSpec((tb, LANE), lambda b, g, j: (b * nb + j, g)),
                  pl.BlockSpec((seq, LANE), lambda b, g, j: (b, g)),
                  pl.BlockSpec((None, None, nb, LANE, tb), lambda b, g, j: (b, g, 0, 0, 0)),
                  pl.BlockSpec((None, LANE, LANE), lambda b, g, j: (b, 0, g))],
        out_specs=pl.BlockSpec((tb, LANE), lambda b, g, j: (b * nb + j, g)),
        out_shape=jax.ShapeDtypeStruct((t, B_W), F32),
        scratch_shapes=[pltpu.VMEM((2, LANE, tb), F32)] + _attn_scratch(tb),
        compiler_params=_cparams(("parallel", "parallel", "arbitrary"), VMEM_LIMIT),
        name="moba_attention",
    )(_alibi_slopes(B_HEADS), qn, kn, _value_blocks_t(z, v_blk0, ngrp, bsz, seq), km)


def diff_attention(qn, kn, z, v_blk0, lam_p, onorm_g, lam_init, bsz, seq):
    t = bsz * seq
    tb = ATT_BLOCK
    nb = seq // tb
    return pl.pallas_call(
        functools.partial(_attn_kernel, moba=False, lam_init=lam_init),
        grid=(bsz, C_HEADS, nb),
        in_specs=[pl.BlockSpec(memory_space=pltpu.SMEM),
                  pl.BlockSpec((tb, LANE), lambda b, g, j: (b * nb + j, g)),
                  pl.BlockSpec((seq, LANE), lambda b, g, j: (b, g)),
                  pl.BlockSpec((None, None, nb, LANE, tb), lambda b, g, j: (b, g, 0, 0, 0)),
                  pl.BlockSpec((4, C_DH), lambda b, g, j: (0, 0)),
                  pl.BlockSpec((1, C_VDH), lambda b, g, j: (0, 0))],
        out_specs=pl.BlockSpec((tb, LANE), lambda b, g, j: (b * nb + j, g)),
        out_shape=jax.ShapeDtypeStruct((t, C_HEADS * C_VDH), F32),
        scratch_shapes=_attn_scratch(tb),
        compiler_params=_cparams(("parallel", "parallel", "arbitrary"), VMEM_LIMIT),
        name="diff_attention",
    )(_alibi_slopes(C_HEADS), qn, kn, _value_blocks_t(z, v_blk0, C_HEADS, bsz, seq), lam_p,
      onorm_g.reshape(1, C_VDH))


def _top_rows(s, count):
    n = s.shape[0]
    rows = lax.broadcasted_iota(jnp.int32, s.shape, 0)
    vals, idxs = [], []
    for _ in range(count):
        mx = jnp.max(s, axis=0, keepdims=True)
        ix = jnp.min(jnp.where(s == mx, rows, n), axis=0, keepdims=True)
        vals.append(mx)
        idxs.append(ix)
        s = jnp.where(rows == ix, NEG_INF, s)
    return jnp.concatenate(vals, axis=0), jnp.concatenate(idxs, axis=0)


def _peer_route_kernel(q_ref, keys_ref, eid_ref, gate_ref):
    kk = PEER_TOPK
    half = PEER_DK // 2
    tops = []
    for p in range(2):
        qp = q_ref[:, p * half:(p + 1) * half]
        st = lax.dot_general(keys_ref[p], qp, (((1,), (1,)), ((), ())), precision=HIGHEST,
                             preferred_element_type=F32)
        tops.append(_top_rows(st, kk))
    (s0, i0), (s1, i1) = tops
    widths = [kk // (a + 1) for a in range(kk)]
    pad = -sum(widths) % 8
    cand_s = jnp.concatenate([s0[a:a + 1] + s1[0:widths[a]] for a in range(kk)]
                             + [jnp.full((pad, s0.shape[1]), NEG_INF, F32)], axis=0)
    cand_i = jnp.concatenate([i0[a:a + 1] * PEER_NKEYS + i1[0:widths[a]] for a in range(kk)]
                             + [jnp.zeros((pad, s0.shape[1]), jnp.int32)], axis=0)
    score, pos = _top_rows(cand_s, kk)
    rows = lax.broadcasted_iota(jnp.int32, cand_i.shape, 0)
    eid = jnp.concatenate(
        [jnp.sum(jnp.where(rows == pos[a:a + 1], cand_i, 0), axis=0, keepdims=True) for a in range(kk)], axis=0)
    e = jnp.exp(score - score[0:1])
    eid_ref[...] = eid
    gate_ref[...] = e / jnp.sum(e, axis=0, keepdims=True)


def peer_route(qry, keys, *, tt=256):
    t = qry.shape[0]
    hk = PEER_HEADS * PEER_TOPK
    return pl.pallas_call(
        _peer_route_kernel,
        grid=(t // tt, PEER_HEADS),
        in_specs=[pl.BlockSpec((tt, PEER_DK), lambda i, h: (i, h)),
                  pl.BlockSpec((2, PEER_NKEYS, PEER_DK // 2), lambda i, h: (0, 0, 0))],
        out_specs=[pl.BlockSpec((PEER_TOPK, tt), lambda i, h: (h, i)),
                   pl.BlockSpec((PEER_TOPK, tt), lambda i, h: (h, i))],
        out_shape=[jax.ShapeDtypeStruct((hk, t), jnp.int32), jax.ShapeDtypeStruct((hk, t), F32)],
        compiler_params=_cparams(("parallel", "parallel")),
        name="peer_route",
    )(qry, keys)


ROW_SUB = 4
PEER_TT = 128


PEER_HK = PEER_HEADS * PEER_TOPK
ROW_BF = 2 * ROW_SUB
GROUP = 8


def pack_table(tab):
    n, d = tab.shape
    b = lax.bitcast_convert_type(tab.astype(BF16), jnp.uint16).astype(jnp.uint32)
    b = b.reshape(n, ROW_SUB, 2, LANE)
    return b[:, :, 0, :] | (b[:, :, 1, :] << 16)


def _split_bf16(x):
    hi = x.astype(BF16)
    lo = (x - hi.astype(F32)).astype(BF16)
    return jnp.concatenate([hi, lo], axis=0)


def _gelu(x):
    return 0.5 * x * (1.0 + lax.erf(x * (2.0 ** -0.5)))


def _gather_rows(idx_refs, tab_ref, dst_ref, t):
    for k in range(PEER_HK):
        dst_ref[k * ROW_SUB:(k + 1) * ROW_SUB, :] = tab_ref[idx_refs[k // PEER_TOPK][k % PEER_TOPK, t]]


def _diag_mask(rows):
    shape = (rows, PEER_HK * ROW_BF)
    return (lax.broadcasted_iota(jnp.int32, shape, 1) % ROW_BF) == (lax.broadcasted_iota(jnp.int32, shape, 0) % ROW_BF)


def _peer_up_kernel(*refs):
    idx_refs = refs[:PEER_HEADS]
    h_ref, gate_ref, tab_ref, act_ref, g0_sc, g1_sc, c_sc = refs[PEER_HEADS:]
    slots = (g0_sc, g1_sc)
    wide = PEER_HK * ROW_BF
    mask = _diag_mask(2 * ROW_BF)
    fold = (lax.broadcasted_iota(jnp.int32, (wide, PEER_HK), 0) // ROW_BF
            == lax.broadcasted_iota(jnp.int32, (wide, PEER_HK), 1)).astype(BF16)

    def group(g, carry):
        t0 = g * GROUP
        _gather_rows(idx_refs,tab_ref, slots[0], t0)
        for i in range(GROUP):
            if i + 1 < GROUP:
                _gather_rows(idx_refs,tab_ref, slots[(i + 1) % 2], t0 + i + 1)
            rows = pltpu.bitcast(slots[i % 2][...], BF16)
            xs = _split_bf16(h_ref[t0 + i])
            y = lax.dot_general(xs, rows, (((1,), (1,)), ((), ())), preferred_element_type=F32)
            y = jnp.where(mask, y, 0.0)
            c_sc[i:i + 1, :] = jnp.sum(y, axis=0, keepdims=True)
        s = jnp.dot(_split_bf16(c_sc[...]), fold, preferred_element_type=F32)
        s = s[0:GROUP] + s[GROUP:]
        rows8 = pl.ds(pl.multiple_of(t0, GROUP), GROUP)
        act_ref[rows8, :] = _gelu(s) * gate_ref[rows8, :]
        return carry

    lax.fori_loop(0, PEER_TT // GROUP, group, 0)


def _peer_down_kernel(*refs):
    idx_refs = refs[:PEER_HEADS]
    act_ref, x_ref, g_ref, tab_ref, out_ref, g0_sc, g1_sc = refs[PEER_HEADS:]
    slots = (g0_sc, g1_sc)
    wide = PEER_HK * ROW_BF
    mask = _diag_mask(ROW_BF)
    spread = (lax.broadcasted_iota(jnp.int32, (PEER_HK, wide), 1) // ROW_BF
              == lax.broadcasted_iota(jnp.int32, (PEER_HK, wide), 0)).astype(BF16)
    gate = g_ref[...]

    def group(g, carry):
        t0 = g * GROUP
        _gather_rows(idx_refs,tab_ref, slots[0], t0)
        acts = act_ref[pl.ds(pl.multiple_of(t0, GROUP), GROUP), :]
        a_wide = jnp.dot(_split_bf16(acts), spread, preferred_element_type=F32)
        for i in range(GROUP):
            if i + 1 < GROUP:
                _gather_rows(idx_refs,tab_ref, slots[(i + 1) % 2], t0 + i + 1)
            rows = pltpu.bitcast(slots[i % 2][...], BF16)
            a_hi = jnp.where(mask, jnp.broadcast_to(a_wide[i:i + 1], (ROW_BF, wide)), 0.0)
            a_lo = jnp.where(mask, jnp.broadcast_to(a_wide[GROUP + i:GROUP + i + 1], (ROW_BF, wide)), 0.0)
            lhs = jnp.concatenate([a_hi, a_lo], axis=0).astype(BF16)
            y = jnp.dot(lhs, rows, preferred_element_type=F32)
            out_ref[t0 + i] = x_ref[t0 + i] + gate * (y[0:ROW_BF] + y[ROW_BF:])
        return carry

    lax.fori_loop(0, PEER_TT // GROUP, group, 0)


def _table_spec(n):
    return pl.BlockSpec((n, ROW_SUB, LANE), lambda i: (0, 0, 0), pipeline_mode=pl.Buffered(1))


def _idx_specs(tt):
    return [pl.BlockSpec((PEER_TOPK, tt), lambda i: (0, i), memory_space=pltpu.SMEM) for _ in range(PEER_HEADS)]


def peer_up(idxs, h, gate, tab_u):
    t, hk = gate.shape
    n = tab_u.shape[0]
    tt = PEER_TT
    return pl.pallas_call(
        _peer_up_kernel,
        grid=(t // tt,),
        in_specs=_idx_specs(tt) + [
                  pl.BlockSpec((tt, 8, LANE), lambda i: (i, 0, 0)),
                  pl.BlockSpec((tt, hk), lambda i: (i, 0)),
                  _table_spec(n)],
        out_specs=pl.BlockSpec((tt, hk), lambda i: (i, 0)),
        out_shape=jax.ShapeDtypeStruct((t, hk), F32),
        scratch_shapes=[pltpu.VMEM((hk * ROW_SUB, LANE), jnp.uint32), pltpu.VMEM((hk * ROW_SUB, LANE), jnp.uint32),
                        pltpu.VMEM((GROUP, hk * ROW_BF), F32)],
        compiler_params=_cparams(("arbitrary",), VMEM_LIMIT),
        name="peer_up",
    )(*idxs, h.reshape(t, 8, LANE), gate, tab_u)


def peer_down(idxs, act, x, gate2, tab_v, seq):
    t, hk = act.shape
    n = tab_v.shape[0]
    d = x.shape[1]
    tt = PEER_TT
    tiles_per_batch = seq // tt
    out = pl.pallas_call(
        _peer_down_kernel,
        grid=(t // tt,),
        in_specs=_idx_specs(tt) + [
                  pl.BlockSpec((tt, hk), lambda i: (i, 0)),
                  pl.BlockSpec((tt, 8, LANE), lambda i: (i, 0, 0)),
                  pl.BlockSpec((None, 8, LANE), lambda i: (i // tiles_per_batch, 0, 0)),
                  _table_spec(n)],
        out_specs=pl.BlockSpec((tt, 8, LANE), lambda i: (i, 0, 0)),
        out_shape=jax.ShapeDtypeStruct((t, 8, LANE), F32),
        scratch_shapes=[pltpu.VMEM((hk * ROW_SUB, LANE), jnp.uint32), pltpu.VMEM((hk * ROW_SUB, LANE), jnp.uint32)],
        compiler_params=_cparams(("arbitrary",), VMEM_LIMIT),
        name="peer_down",
    )(*idxs, act, x.reshape(t, 8, LANE), gate2.reshape(-1, 8, LANE), tab_v)
    return out.reshape(t, d)


def peer_ffn(x, g, sc, sh, gate2, wq, keys, tab_u, tab_v, seq):
    qry, h = norm_linear(x, g, sc, sh, wq, jnp.zeros((wq.shape[1],), F32), seq, emit_h=True)
    eid_t, gate_t = peer_route(qry, keys)
    idxs = [eid_t[hd * PEER_TOPK:(hd + 1) * PEER_TOPK] for hd in range(PEER_HEADS)]
    act = peer_up(idxs, h, gate_t.T, tab_u)
    return peer_down(idxs, act, x, gate2, tab_v, seq)


def _even_mixer(xt, g, sc, sh, gate, w_in, conv_w, igate_b, fgate_b, mnorm_g, qn_g, kn_g, w_out, bsz, seq):
    d = xt.shape[1]
    n_gate = 2 * M_HEADS
    g0 = 4 * M_W
    w_main = jnp.concatenate([w_in[:, :g0], w_in[:, g0 + n_gate:]], axis=1).astype(BF16)
    w_gate = jnp.pad(w_in[:, g0:g0 + n_gate], ((0, 0), (0, LANE - n_gate)))
    b_gate = jnp.pad(jnp.concatenate([igate_b, fgate_b]), (0, LANE - n_gate))
    z = norm_linear(xt, g, sc, sh, w_main, jnp.zeros((w_main.shape[1],), F32), seq)
    zg = norm_linear(xt, g, sc, sh, w_gate, b_gate, seq)
    icol, fcol, irow, frow = split_gates(zg)
    hm = mlstm_mixer(z, conv_w, icol, fcol, irow, frow, mnorm_g, bsz, seq)
    blk = g0 // LANE
    nblk = B_W // LANE
    qn, kn, kmean = head_norm(z, blk, blk + nblk, nblk, qn_g, kn_g, True)
    hb = moba_attention(qn, kn, kmean, z, blk + 2 * nblk, bsz, seq)
    w_out = w_out.astype(BF16)
    return linear_residual([hm, hb], [w_out[:M_W], w_out[M_W:]], xt, gate, seq)


def _odd_mixer(xt, g, sc, sh, gate, w_in, qn_g, kn_g, lam_p, onorm_g, w_out, lam_init, bsz, seq):
    z = norm_linear(xt, g, sc, sh, w_in.astype(BF16), jnp.zeros((w_in.shape[1],), F32), seq)
    nblk = 2 * C_HEADS * C_DH // LANE
    qn, kn = head_norm(z, 0, nblk, nblk, qn_g, kn_g, False)
    o = diff_attention(qn, kn, z, 2 * nblk, lam_p, onorm_g, lam_init, bsz, seq)
    return linear_residual([o], [w_out.astype(BF16)], xt, gate, seq)


def kernel(x, c, ada_w, ada_b, norm_mix_g, norm_ffn_g, ev_w_in, ev_conv_w, ev_igate_b, ev_fgate_b, ev_mnorm_g,
           ev_qn_g, ev_kn_g, ev_w_out, od_w_in, od_qn_g, od_kn_g, od_lam, od_onorm_g, od_w_out, peer_wq,
           peer_keys, peer_u, peer_v):
    bsz, seq, d = x.shape
    depth = ada_w.shape[0]
    mod = adaln_mod(c, ada_w, ada_b).reshape(depth, bsz, 6, 1, d)
    xt = x.reshape(bsz * seq, d)
    for layer in range(depth):
        sh1, sc1, g1, sh2, sc2, g2 = [mod[layer, :, i] for i in range(6)]
        if layer % 2 == 0:
            e = layer // 2
            xt = _even_mixer(xt, norm_mix_g[layer], sc1, sh1, g1, ev_w_in[e], ev_conv_w[e], ev_igate_b[e],
                             ev_fgate_b[e], ev_mnorm_g[e], ev_qn_g[e], ev_kn_g[e], ev_w_out[e], bsz, seq)
        else:
            o = layer // 2
            lam_init = 0.8 - 0.6 * math.exp(-0.3 * layer)
            xt = _odd_mixer(xt, norm_mix_g[layer], sc1, sh1, g1, od_w_in[o], od_qn_g[o], od_kn_g[o], od_lam[o],
                            od_onorm_g[o], od_w_out[o], lam_init, bsz, seq)
        xt = peer_ffn(xt, norm_ffn_g[layer], sc2, sh2, g2, peer_wq[layer].astype(BF16), peer_keys[layer],
                      pack_table(peer_u[layer]), pack_table(peer_v[layer]), seq)
    return xt.reshape(bsz, seq, d)
```

```python
import functools
import math

import jax
import jax.numpy as jnp
from jax import lax
from jax.experimental import pallas as pl
from jax.experimental.pallas import tpu as pltpu

F32 = jnp.float32
BF16 = jnp.bfloat16
HIGHEST = lax.Precision.HIGHEST
NEG_INF = float("-inf")

EPS = 1e-6
D_MODEL = 1024
M_HEADS, M_DH, M_CHUNK, CONV_K = 4, 128, 128, 4
M_W = M_HEADS * M_DH
B_HEADS, B_DH, MOBA_BLOCK, MOBA_TOPK = 8, 64, 256, 3
B_W = B_HEADS * B_DH
C_HEADS, C_DH, C_VDH = 8, 64, 128
PEER_HEADS, PEER_NKEYS, PEER_DK, PEER_TOPK = 8, 128, 256, 16

LANE = 128
VMEM_LIMIT = 56 * 1024 * 1024


def _cparams(sem, vmem=None):
    return pltpu.CompilerParams(dimension_semantics=sem, vmem_limit_bytes=vmem)


def _mod_kernel(c_ref, w_ref, b_ref, o_ref):
    c = c_ref[...]
    ca = c * (1.0 / (1.0 + jnp.exp(-c)))
    o_ref[...] = jnp.dot(ca, w_ref[...], precision=HIGHEST, preferred_element_type=F32) + b_ref[...]


def adaln_mod(c, ada_w, ada_b):
    depth, d, n = ada_w.shape
    bsz = c.shape[0]
    tn = 512
    return pl.pallas_call(
        _mod_kernel,
        grid=(depth, n // tn),
        in_specs=[pl.BlockSpec((bsz, d), lambda l, j: (0, 0)),
                  pl.BlockSpec((None, d, tn), lambda l, j: (l, 0, j)),
                  pl.BlockSpec((None, 1, tn), lambda l, j: (l, 0, j))],
        out_specs=pl.BlockSpec((None, bsz, tn), lambda l, j: (l, 0, j)),
        out_shape=jax.ShapeDtypeStruct((depth, bsz, n), F32),
        compiler_params=_cparams(("parallel", "parallel")),
        name="adaln_mod",
    )(c, ada_w, ada_b.reshape(depth, 1, n))


def _norm_linear_kernel(x_ref, g_ref, sc_ref, sh_ref, w_ref, b_ref, o_ref, *rest, emit_h, precise):
    if emit_h:
        h_out_ref, h_sc = rest
    else:
        (h_sc,) = rest

    @pl.when(pl.program_id(1) == 0)
    def _():
        x = x_ref[...]
        y = x * lax.rsqrt(jnp.mean(x * x, axis=-1, keepdims=True) + EPS) * g_ref[...]
        h = y * (1.0 + sc_ref[...]) + sh_ref[...]
        h_sc[...] = h.astype(h_sc.dtype)
        if emit_h:
            h_out_ref[...] = h

    if precise:
        acc = jnp.dot(h_sc[...], w_ref[...], precision=HIGHEST, preferred_element_type=F32)
    else:
        acc = jnp.dot(h_sc[...], w_ref[...], preferred_element_type=F32)
    o_ref[...] = (acc + b_ref[...]).astype(o_ref.dtype)


def norm_linear(x, g, sc, sh, w, bias, seq, *, tm=1024, tn=512, emit_h=False, out_dtype=F32):
    t, d = x.shape
    n = w.shape[1]
    tn = min(tn, n)
    tm = min(tm, seq)
    precise = w.dtype == F32
    rows_per_batch = seq // tm
    out_shape = [jax.ShapeDtypeStruct((t, n), out_dtype)]
    out_specs = [pl.BlockSpec((tm, tn), lambda i, j: (i, j))]
    if emit_h:
        out_shape.append(jax.ShapeDtypeStruct((t, d), F32))
        out_specs.append(pl.BlockSpec((tm, d), lambda i, j: (i, 0)))
    res = pl.pallas_call(
        functools.partial(_norm_linear_kernel, emit_h=emit_h, precise=precise),
        grid=(t // tm, n // tn),
        in_specs=[pl.BlockSpec((tm, d), lambda i, j: (i, 0)),
                  pl.BlockSpec((1, d), lambda i, j: (0, 0)),
                  pl.BlockSpec((None, 1, d), lambda i, j: (i // rows_per_batch, 0, 0)),
                  pl.BlockSpec((None, 1, d), lambda i, j: (i // rows_per_batch, 0, 0)),
                  pl.BlockSpec((d, tn), lambda i, j: (0, j)),
                  pl.BlockSpec((1, tn), lambda i, j: (0, j))],
        out_specs=out_specs,
        out_shape=out_shape,
        scratch_shapes=[pltpu.VMEM((tm, d), w.dtype)],
        compiler_params=_cparams(("parallel", "arbitrary"), VMEM_LIMIT),
        name="norm_linear",
    )(x, g.reshape(1, d), sc, sh, w, bias.reshape(1, n))
    return res if emit_h else res[0]


def _linear_residual_kernel(*refs, nparts):
    y_refs, w_refs = refs[:nparts], refs[nparts:2 * nparts]
    x_ref, gate_ref, o_ref = refs[2 * nparts:]
    acc = None
    for y_ref, w_ref in zip(y_refs, w_refs):
        part = jnp.dot(y_ref[...].astype(BF16), w_ref[...], preferred_element_type=F32)
        acc = part if acc is None else acc + part
    o_ref[...] = x_ref[...] + gate_ref[...] * acc


def linear_residual(ys, ws, x, gate, seq, *, tm=512):
    t, d = x.shape
    rows_per_batch = seq // tm
    return pl.pallas_call(
        functools.partial(_linear_residual_kernel, nparts=len(ys)),
        grid=(t // tm,),
        in_specs=([pl.BlockSpec((tm, y.shape[1]), lambda i: (i, 0)) for y in ys]
                  + [pl.BlockSpec(w.shape, lambda i: (0, 0)) for w in ws]
                  + [pl.BlockSpec((tm, d), lambda i: (i, 0)),
                     pl.BlockSpec((None, 1, d), lambda i: (i // rows_per_batch, 0, 0))]),
        out_specs=pl.BlockSpec((tm, d), lambda i: (i, 0)),
        out_shape=jax.ShapeDtypeStruct((t, d), F32),
        compiler_params=_cparams(("parallel",), VMEM_LIMIT),
        name="linear_residual",
    )(*ys, *ws, x, gate)


def _shift_rows(x, tail, s):
    if s == 0:
        return x
    xs = pltpu.roll(x, s, axis=0)
    ts = pltpu.roll(tail, s, axis=0)
    row = lax.broadcasted_iota(jnp.int32, (8, x.shape[1]), 0)
    top = jnp.where(row < s, ts, xs[0:8])
    return jnp.concatenate([top, xs[8:]], axis=0)


def _conv_silu(x, tail, w):
    acc = None
    for j in range(CONV_K):
        term = w[j:j + 1, :] * _shift_rows(x, tail, CONV_K - 1 - j)
        acc = term if acc is None else acc + term
    return acc * (1.0 / (1.0 + jnp.exp(-acc)))


def _log_sigmoid(x):
    return jnp.minimum(x, 0.0) - jnp.log(1.0 + jnp.exp(-jnp.abs(x)))


def _mlstm_kernel(q_ref, k_ref, v_ref, o_ref, wq_ref, wk_ref, icol_ref, fcol_ref, irow_ref, frow_ref,
                  g_ref, out_ref, c_sc, n_sc, m_sc, qt_sc, kt_sc):
    L = M_CHUNK

    @pl.when(pl.program_id(2) == 0)
    def _():
        c_sc[...] = jnp.zeros_like(c_sc)
        n_sc[...] = jnp.zeros_like(n_sc)
        m_sc[...] = jnp.zeros_like(m_sc)
        qt_sc[...] = jnp.zeros_like(qt_sc)
        kt_sc[...] = jnp.zeros_like(kt_sc)

    q_raw = q_ref[...]
    k_raw = k_ref[...]
    q = _conv_silu(q_raw, qt_sc[...], wq_ref[...])
    k = _conv_silu(k_raw, kt_sc[...], wk_ref[...]) * (M_DH ** -0.5)
    qt_sc[...] = q_raw[L - 8:, :]
    kt_sc[...] = k_raw[L - 8:, :]
    v = v_ref[...]

    row = lax.broadcasted_iota(jnp.int32, (L, L), 0)
    col = lax.broadcasted_iota(jnp.int32, (L, L), 1)
    causal = row >= col
    tri = jnp.where(causal, 1.0, 0.0).astype(F32)
    triu = jnp.where(row <= col, 1.0, 0.0).astype(F32)

    lf_col = _log_sigmoid(fcol_ref[...])
    lf_row = _log_sigmoid(frow_ref[...])
    ig_col = icol_ref[...]
    ig_row = irow_ref[...]
    b_col = jnp.dot(tri, jnp.broadcast_to(lf_col, (L, L)), precision=HIGHEST,
                    preferred_element_type=F32)[:, 0:1]
    b_row = jnp.dot(jnp.broadcast_to(lf_row, (8, L)), triu, precision=HIGHEST,
                    preferred_element_type=F32)[0:1, :]
    b_end = b_row[:, L - 1:L]

    m_prev = m_sc[...]
    c_prev = c_sc[...]
    n_prev = n_sc[...]

    log_d = jnp.where(causal, b_col - b_row + ig_row, NEG_INF)
    a_inter = b_col + m_prev
    m_t = jnp.maximum(a_inter, jnp.max(log_d, axis=1, keepdims=True))
    w_inter = jnp.exp(a_inter - m_t)
    qb = q.astype(BF16)
    kb = k.astype(BF16)
    vb = v.astype(BF16)
    s = lax.dot_general(qb, kb, (((1,), (1,)), ((), ())), preferred_element_type=F32)
    s = s * jnp.exp(log_d - m_t)
    num = (w_inter * jnp.dot(qb, c_prev.astype(BF16), preferred_element_type=F32)
           + jnp.dot(s.astype(BF16), vb, preferred_element_type=F32))
    den = (w_inter * jnp.sum(q * n_prev, axis=1, keepdims=True) + jnp.sum(s, axis=1, keepdims=True))
    h = num / jnp.maximum(jnp.abs(den), jnp.exp(-m_t))

    w_end_row = b_end - b_row + ig_row
    m_loc = jnp.max(w_end_row, axis=1, keepdims=True)
    e_col = jnp.exp(b_end - b_col + ig_col - m_loc)
    ke = k * e_col
    c_loc = lax.dot_general(ke.astype(BF16), vb, (((0,), (0,)), ((), ())), preferred_element_type=F32)
    n_loc = jnp.sum(ke, axis=0, keepdims=True)
    m_new = jnp.maximum(b_end + m_prev, m_loc)
    a = jnp.exp(b_end + m_prev - m_new)
    sc = jnp.exp(m_loc - m_new)
    c_sc[...] = a * c_prev + sc * c_loc
    n_sc[...] = a * n_prev + sc * n_loc
    m_sc[...] = m_new

    hn = h * lax.rsqrt(jnp.mean(h * h, axis=1, keepdims=True) + EPS) * g_ref[...]
    og = o_ref[...]
    out_ref[...] = (hn * (1.0 / (1.0 + jnp.exp(-og)))).astype(out_ref.dtype)


def mlstm_mixer(z, conv_w, icol, fcol, irow, frow, mnorm_g, bsz, seq):
    t = bsz * seq
    L = M_CHUNK
    nc = seq // L
    hq = M_W // LANE

    def zspec(sec):
        return pl.BlockSpec((L, M_DH), lambda b, h, c, sec=sec: (b * nc + c, sec * hq + h))

    return pl.pallas_call(
        _mlstm_kernel,
        grid=(bsz, M_HEADS, nc),
        in_specs=[zspec(0), zspec(1), zspec(2), zspec(3),
                  pl.BlockSpec((CONV_K, M_DH), lambda b, h, c: (0, h)),
                  pl.BlockSpec((CONV_K, M_DH), lambda b, h, c: (0, hq + h)),
                  pl.BlockSpec((None, L, 1), lambda b, h, c: (h, b * nc + c, 0)),
                  pl.BlockSpec((None, L, 1), lambda b, h, c: (h, b * nc + c, 0)),
                  pl.BlockSpec((None, None, 1, L), lambda b, h, c: (h, b * nc + c, 0, 0)),
                  pl.BlockSpec((None, None, 1, L), lambda b, h, c: (h, b * nc + c, 0, 0)),
                  pl.BlockSpec((None, 1, M_DH), lambda b, h, c: (h, 0, 0))],
        out_specs=pl.BlockSpec((L, M_DH), lambda b, h, c: (b * nc + c, h)),
        out_shape=jax.ShapeDtypeStruct((t, M_W), F32),
        scratch_shapes=[pltpu.VMEM((M_DH, M_DH), F32), pltpu.VMEM((1, M_DH), F32), pltpu.VMEM((1, 1), F32),
                        pltpu.VMEM((8, M_DH), F32), pltpu.VMEM((8, M_DH), F32)],
        compiler_params=_cparams(("parallel", "parallel", "arbitrary")),
        name="mlstm",
    )(z, z, z, z, conv_w, conv_w, icol, fcol, irow, frow, mnorm_g.reshape(M_HEADS, 1, M_DH))


def split_gates(zg):
    t = zg.shape[0]
    gt = zg[:, :2 * M_HEADS].T
    cols = gt.reshape(2 * M_HEADS, t, 1)
    rows = gt.reshape(2 * M_HEADS, t // M_CHUNK, 1, M_CHUNK)
    return cols[:M_HEADS], cols[M_HEADS:], rows[:M_HEADS], rows[M_HEADS:]


ATT_BLOCK = 256
ATT_UNROLL = 4
HALF = 64


def _half_rms_norm(x, g):
    lo = lax.broadcasted_iota(jnp.int32, x.shape, 1) < HALF
    x2 = x * x
    s_lo = jnp.sum(jnp.where(lo, x2, 0.0), axis=1, keepdims=True)
    s_hi = jnp.sum(jnp.where(lo, 0.0, x2), axis=1, keepdims=True)
    r = jnp.where(lo, lax.rsqrt(s_lo * (1.0 / HALF) + EPS), lax.rsqrt(s_hi * (1.0 / HALF) + EPS))
    return x * r * g


def _head_norm_kernel(q_ref, k_ref, gq_ref, gk_ref, qn_ref, kn_ref, *rest):
    for c in range(q_ref.shape[1] // LANE):
        cols = slice(c * LANE, (c + 1) * LANE)
        qn_ref[:, cols] = _half_rms_norm(q_ref[:, cols], gq_ref[...])
        kn = _half_rms_norm(k_ref[:, cols], gk_ref[...])
        kn_ref[:, cols] = kn.astype(kn_ref.dtype)
        if rest:
            rest[0][:, cols] = jnp.mean(kn, axis=0, keepdims=True)


def head_norm(z, q_blk0, k_blk0, nblk, gq, gk, with_kmean):
    t = z.shape[0]
    w = nblk * LANE
    tb = ATT_BLOCK
    gq2 = jnp.concatenate([gq, gq]).reshape(1, LANE)
    gk2 = jnp.concatenate([gk, gk]).reshape(1, LANE)
    assert q_blk0 % nblk == 0 and k_blk0 % nblk == 0
    out_shape = [jax.ShapeDtypeStruct((t, w), F32), jax.ShapeDtypeStruct((t, w), BF16)]
    out_specs = [pl.BlockSpec((tb, w), lambda i: (i, 0)), pl.BlockSpec((tb, w), lambda i: (i, 0))]
    if with_kmean:
        out_shape.append(jax.ShapeDtypeStruct((t // tb, 1, w), F32))
        out_specs.append(pl.BlockSpec((None, 1, w), lambda i: (i, 0, 0)))
    return pl.pallas_call(
        _head_norm_kernel,
        grid=(t // tb,),
        in_specs=[pl.BlockSpec((tb, w), lambda i: (i, q_blk0 // nblk)),
                  pl.BlockSpec((tb, w), lambda i: (i, k_blk0 // nblk)),
                  pl.BlockSpec((1, LANE), lambda i: (0, 0)),
                  pl.BlockSpec((1, LANE), lambda i: (0, 0))],
        out_specs=out_specs,
        out_shape=out_shape,
        compiler_params=_cparams(("parallel",)),
        name="head_norm",
    )(z, z, gq2, gk2)


def _attn_kernel(*refs, moba, lam_init):
    if moba:
        (slope_ref, q_ref, k_ref, vt_ref, km_ref, out_ref, sel_sc, m_sc, l_sc, acc_sc) = refs
    else:
        (slope_ref, q_ref, k_ref, vt_ref, lam_ref, g_ref, out_ref, m_sc, l_sc, acc_sc) = refs
    tb = ATT_BLOCK
    grp = pl.program_id(1)
    j = pl.program_id(2)
    scale = HALF ** -0.5

    q = q_ref[...]
    lo = lax.broadcasted_iota(jnp.int32, (tb, LANE), 1) < HALF
    qs = q * scale
    q_half = [jnp.where(lo, qs, 0.0).astype(BF16), jnp.where(lo, 0.0, qs).astype(BF16)]
    krow = lax.broadcasted_iota(jnp.int32, (tb, tb), 0)
    causal = krow <= lax.broadcasted_iota(jnp.int32, (tb, tb), 1)
    if moba:
        slopes = [slope_ref[2 * grp], slope_ref[2 * grp + 1]]
    else:
        slopes = [slope_ref[grp], slope_ref[grp]]
    kpos = krow.astype(F32)
    bias = [slopes[0] * kpos, slopes[1] * kpos] if moba else [slopes[0] * kpos] * 2

    if moba:
        km = km_ref[...]
        blk = lax.broadcasted_iota(jnp.int32, (LANE, tb), 0)
        for hf in range(2):
            qm = jnp.where(lo, q, 0.0) if hf == 0 else jnp.where(lo, 0.0, q)
            gate = lax.dot_general(km, qm, (((1,), (1,)), ((), ())), precision=HIGHEST,
                                   preferred_element_type=F32)
            gate = jnp.where(blk < j, gate, NEG_INF)
            sel = jnp.zeros((LANE, tb), F32)
            for _ in range(MOBA_TOPK):
                mx = jnp.max(gate, axis=0, keepdims=True)
                idx = jnp.min(jnp.where(gate == mx, blk, LANE), axis=0, keepdims=True)
                hit = blk == idx
                sel = jnp.where(jnp.logical_and(hit, mx > NEG_INF), 1.0, sel)
                gate = jnp.where(hit, NEG_INF, gate)
            sel_sc[hf] = sel

    m_sc[...] = jnp.full(m_sc.shape, NEG_INF, F32)
    l_sc[...] = jnp.zeros(l_sc.shape, F32)
    acc_sc[...] = jnp.zeros(acc_sc.shape, F32)

    def block_update(blocks, diag):
        m_old = [m_sc[0], m_sc[1]]
        l_old = [l_sc[0], l_sc[1]]
        acc_old = [acc_sc[0], acc_sc[1]]
        xs, shifts, vts = [], [], []
        for n in blocks:
            start = pl.multiple_of(n * tb, tb)
            kb = k_ref[pl.ds(start, tb), :]
            vts.append(vt_ref[n])
            off = ((n - j) * tb).astype(F32)
            for hf in range(2):
                x = lax.dot_general(kb, q_half[hf], (((1,), (1,)), ((), ())), preferred_element_type=F32)
                x = x + bias[hf]
                if diag:
                    x = jnp.where(causal, x, NEG_INF)
                elif moba:
                    x = jnp.where(sel_sc[hf, pl.ds(n, 1), :] > 0.5, x, NEG_INF)
                xs.append(x)
                shifts.append(slopes[hf] * off)
        for hf in range(2):
            ids = [2 * b + hf for b in range(len(blocks))]
            m_new = m_old[hf]
            for i in ids:
                m_new = jnp.maximum(m_new, jnp.max(xs[i], axis=0, keepdims=True) + shifts[i])
            alpha = jnp.exp(m_old[hf] - m_new)
            l_new = alpha * l_old[hf]
            acc_new = alpha * acc_old[hf]
            for b, i in enumerate(ids):
                p = jnp.exp(xs[i] - (m_new - shifts[i]))
                l_new = l_new + jnp.sum(p, axis=0, keepdims=True)
                acc_new = acc_new + jnp.dot(vts[b], p.astype(BF16), preferred_element_type=F32)
            l_sc[hf] = l_new
            acc_sc[hf] = acc_new
            m_sc[hf] = m_new

    block_update([j], True)

    def body(i, carry):
        block_update([ATT_UNROLL * i + u for u in range(ATT_UNROLL)], False)
        return carry

    lax.fori_loop(0, j // ATT_UNROLL, body, 0)

    done = (j // ATT_UNROLL) * ATT_UNROLL
    width = ATT_UNROLL // 2
    while width >= 1:
        has = (j & width) != 0

        @pl.when(has)
        def _(done=done, width=width):
            block_update([done + u for u in range(width)], False)

        done = done + jnp.where(has, width, 0)
        width //= 2

    o0 = acc_sc[0] * (1.0 / l_sc[0])
    o1 = acc_sc[1] * (1.0 / l_sc[1])
    if moba:
        first_head = lax.broadcasted_iota(jnp.int32, (LANE, tb), 0) < HALF
        out_ref[...] = jnp.where(first_head, o0, o1).T.astype(out_ref.dtype)
    else:
        lp = lam_ref[...]
        s1 = jnp.sum(lp[0:1] * lp[1:2], axis=1, keepdims=True)
        s2 = jnp.sum(lp[2:3] * lp[3:4], axis=1, keepdims=True)
        lam = jnp.exp(s1) - jnp.exp(s2) + lam_init
        o = (o0 - lam * o1).T
        o = o * lax.rsqrt(jnp.mean(o * o, axis=1, keepdims=True) + EPS) * g_ref[...]
        out_ref[...] = (o * (1.0 - lam_init)).astype(out_ref.dtype)


def _alibi_slopes(n_heads):
    return jnp.asarray([2.0 ** (-8.0 * (h + 1) / n_heads) for h in range(n_heads)], dtype=F32)


def _value_blocks_t(z, v_blk0, ngrp, bsz, seq):
    nb = seq // ATT_BLOCK
    v = z[:, v_blk0 * LANE:(v_blk0 + ngrp) * LANE].astype(BF16)
    return v.reshape(bsz, nb, ATT_BLOCK, ngrp, LANE).transpose(0, 3, 1, 4, 2)


def _attn_scratch(tb):
    return [pltpu.VMEM((2, 1, tb), F32), pltpu.VMEM((2, 1, tb), F32), pltpu.VMEM((2, LANE, tb), F32)]


def moba_attention(qn, kn, kmean, z, v_blk0, bsz, seq):
    t = bsz * seq
    tb = ATT_BLOCK
    nb = seq // tb
    ngrp = B_W // LANE
    km = jnp.pad(kmean.reshape(bsz, nb, B_W), ((0, 0), (0, LANE - nb), (0, 0)))
    return pl.pallas_call(
        functools.partial(_attn_kernel, moba=True, lam_init=None),
        grid=(bsz, ngrp, nb),
        in_specs=[pl.BlockSpec(memory_space=pltpu.SMEM),
                  pl.BlockSpec((tb, LANE), lambda b, g, j: (b * nb + j, g)),
                  pl.BlockSpec((seq, LANE), lambda b, g, j: (b, g)),
                  pl.BlockSpec((None, None, nb, LANE, tb), lambda b, g, j: (b, g, 0, 0, 0)),
                  pl.BlockSpec((None, LANE, LANE), lambda b, g, j: (b, 0, g))],
        out_specs=pl.BlockSpec((tb, LANE), lambda b, g, j: (b * nb + j, g)),
        out_shape=jax.ShapeDtypeStruct((t, B_W), F32),
        scratch_shapes=[pltpu.VMEM((2, LANE, tb), F32)] + _attn_scratch(tb),
        compiler_params=_cparams(("parallel", "parallel", "arbitrary"), VMEM_LIMIT),
        name="moba_attention",
    )(_alibi_slopes(B_HEADS), qn, kn, _value_blocks_t(z, v_blk0, ngrp, bsz, seq), km)


def diff_attention(qn, kn, z, v_blk0, lam_p, onorm_g, lam_init, bsz, seq):
    t = bsz * seq
    tb = ATT_BLOCK
    nb = seq // tb
    return pl.pallas_call(
        functools.partial(_attn_kernel, moba=False, lam_init=lam_init),
        grid=(bsz, C_HEADS, nb),
        in_specs=[pl.BlockSpec(memory_space=pltpu.SMEM),
                  pl.BlockSpec((tb, LANE), lambda b, g, j: (b * nb + j, g)),
                  pl.BlockSpec((seq, LANE), lambda b, g, j: (b, g)),
                  pl.BlockSpec((None, None, nb, LANE, tb), lambda b, g, j: (b, g, 0, 0, 0)),
                  pl.BlockSpec((4, C_DH), lambda b, g, j: (0, 0)),
                  pl.BlockSpec((1, C_VDH), lambda b, g, j: (0, 0))],
        out_specs=pl.BlockSpec((tb, LANE), lambda b, g, j: (b * nb + j, g)),
        out_shape=jax.ShapeDtypeStruct((t, C_HEADS * C_VDH), F32),
        scratch_shapes=_attn_scratch(tb),
        compiler_params=_cparams(("parallel", "parallel", "arbitrary"), VMEM_LIMIT),
        name="diff_attention",
    )(_alibi_slopes(C_HEADS), qn, kn, _value_blocks_t(z, v_blk0, C_HEADS, bsz, seq), lam_p,
      onorm_g.reshape(1, C_VDH))


def _top_rows(s, count):
    n = s.shape[0]
    rows = lax.broadcasted_iota(jnp.int32, s.shape, 0)
    vals, idxs = [], []
    for _ in range(count):
        mx = jnp.max(s, axis=0, keepdims=True)
        ix = jnp.min(jnp.where(s == mx, rows, n), axis=0, keepdims=True)
        vals.append(mx)
        idxs.append(ix)
        s = jnp.where(rows == ix, NEG_INF, s)
    return jnp.concatenate(vals, axis=0), jnp.concatenate(idxs, axis=0)


def _peer_route_kernel(q_ref, keys_ref, eid_ref, gate_ref):
    kk = PEER_TOPK
    half = PEER_DK // 2
    tops = []
    for p in range(2):
        qp = q_ref[:, p * half:(p + 1) * half]
        st = lax.dot_general(keys_ref[p], qp, (((1,), (1,)), ((), ())), precision=HIGHEST,
                             preferred_element_type=F32)
        tops.append(_top_rows(st, kk))
    (s0, i0), (s1, i1) = tops
    widths = [kk // (a + 1) for a in range(kk)]
    pad = -sum(widths) % 8
    cand_s = jnp.concatenate([s0[a:a + 1] + s1[0:widths[a]] for a in range(kk)]
                             + [jnp.full((pad, s0.shape[1]), NEG_INF, F32)], axis=0)
    cand_i = jnp.concatenate([i0[a:a + 1] * PEER_NKEYS + i1[0:widths[a]] for a in range(kk)]
                             + [jnp.zeros((pad, s0.shape[1]), jnp.int32)], axis=0)
    score, pos = _top_rows(cand_s, kk)
    rows = lax.broadcasted_iota(jnp.int32, cand_i.shape, 0)
    eid = jnp.concatenate(
        [jnp.sum(jnp.where(rows == pos[a:a + 1], cand_i, 0), axis=0, keepdims=True) for a in range(kk)], axis=0)
    e = jnp.exp(score - score[0:1])
    eid_ref[...] = eid
    gate_ref[...] = e / jnp.sum(e, axis=0, keepdims=True)


def peer_route(qry, keys, *, tt=256):
    t = qry.shape[0]
    hk = PEER_HEADS * PEER_TOPK
    return pl.pallas_call(
        _peer_route_kernel,
        grid=(t // tt, PEER_HEADS),
        in_specs=[pl.BlockSpec((tt, PEER_DK), lambda i, h: (i, h)),
                  pl.BlockSpec((2, PEER_NKEYS, PEER_DK // 2), lambda i, h: (0, 0, 0))],
        out_specs=[pl.BlockSpec((PEER_TOPK, tt), lambda i, h: (h, i)),
                   pl.BlockSpec((PEER_TOPK, tt), lambda i, h: (h, i))],
        out_shape=[jax.ShapeDtypeStruct((hk, t), jnp.int32), jax.ShapeDtypeStruct((hk, t), F32)],
        compiler_params=_cparams(("parallel", "parallel")),
        name="peer_route",
    )(qry, keys)


ROW_SUB = 4
PEER_TT = 128


PEER_HK = PEER_HEADS * PEER_TOPK
ROW_BF = 2 * ROW_SUB
GROUP = 8


def pack_table(tab):
    n, d = tab.shape
    b = lax.bitcast_convert_type(tab.astype(BF16), jnp.uint16).astype(jnp.uint32)
    b = b.reshape(n, ROW_SUB, 2, LANE)
    return b[:, :, 0, :] | (b[:, :, 1, :] << 16)


def _split_bf16(x):
    hi = x.astype(BF16)
    lo = (x - hi.astype(F32)).astype(BF16)
    return jnp.concatenate([hi, lo], axis=0)


def _gelu(x):
    return 0.5 * x * (1.0 + lax.erf(x * (2.0 ** -0.5)))


def _gather_rows(idx_refs, tab_ref, dst_ref, t):
    for k in range(PEER_HK):
        dst_ref[k * ROW_SUB:(k + 1) * ROW_SUB, :] = tab_ref[idx_refs[k // PEER_TOPK][k % PEER_TOPK, t]]


def _diag_mask(rows):
    shape = (rows, PEER_HK * ROW_BF)
    return (lax.broadcasted_iota(jnp.int32, shape, 1) % ROW_BF) == (lax.broadcasted_iota(jnp.int32, shape, 0) % ROW_BF)


def _peer_up_kernel(*refs):
    idx_refs = refs[:PEER_HEADS]
    h_ref, gate_ref, tab_ref, act_ref, g0_sc, g1_sc, c_sc = refs[PEER_HEADS:]
    slots = (g0_sc, g1_sc)
    wide = PEER_HK * ROW_BF
    mask = _diag_mask(2 * ROW_BF)
    fold = (lax.broadcasted_iota(jnp.int32, (wide, PEER_HK), 0) // ROW_BF
            == lax.broadcasted_iota(jnp.int32, (wide, PEER_HK), 1)).astype(BF16)

    def group(g, carry):
        t0 = g * GROUP
        _gather_rows(idx_refs,tab_ref, slots[0], t0)
        for i in range(GROUP):
            if i + 1 < GROUP:
                _gather_rows(idx_refs,tab_ref, slots[(i + 1) % 2], t0 + i + 1)
            rows = pltpu.bitcast(slots[i % 2][...], BF16)
            xs = _split_bf16(h_ref[t0 + i])
            y = lax.dot_general(xs, rows, (((1,), (1,)), ((), ())), preferred_element_type=F32)
            y = jnp.where(mask, y, 0.0)
            c_sc[i:i + 1, :] = jnp.sum(y, axis=0, keepdims=True)
        s = jnp.dot(_split_bf16(c_sc[...]), fold, preferred_element_type=F32)
        s = s[0:GROUP] + s[GROUP:]
        rows8 = pl.ds(pl.multiple_of(t0, GROUP), GROUP)
        act_ref[rows8, :] = _gelu(s) * gate_ref[rows8, :]
        return carry

    lax.fori_loop(0, PEER_TT // GROUP, group, 0)


def _peer_down_kernel(*refs):
    idx_refs = refs[:PEER_HEADS]
    act_ref, x_ref, g_ref, tab_ref, out_ref, g0_sc, g1_sc = refs[PEER_HEADS:]
    slots = (g0_sc, g1_sc)
    wide = PEER_HK * ROW_BF
    mask = _diag_mask(ROW_BF)
    spread = (lax.broadcasted_iota(jnp.int32, (PEER_HK, wide), 1) // ROW_BF
              == lax.broadcasted_iota(jnp.int32, (PEER_HK, wide), 0)).astype(BF16)
    gate = g_ref[...]

    def group(g, carry):
        t0 = g * GROUP
        _gather_rows(idx_refs,tab_ref, slots[0], t0)
        acts = act_ref[pl.ds(pl.multiple_of(t0, GROUP), GROUP), :]
        a_wide = jnp.dot(_split_bf16(acts), spread, preferred_element_type=F32)
        for i in range(GROUP):
            if i + 1 < GROUP:
                _gather_rows(idx_refs,tab_ref, slots[(i + 1) % 2], t0 + i + 1)
            rows = pltpu.bitcast(slots[i % 2][...], BF16)
            a_hi = jnp.where(mask, jnp.broadcast_to(a_wide[i:i + 1], (ROW_BF, wide)), 0.0)
            a_lo = jnp.where(mask, jnp.broadcast_to(a_wide[GROUP + i:GROUP + i + 1], (ROW_BF, wide)), 0.0)
            lhs = jnp.concatenate([a_hi, a_lo], axis=0).astype(BF16)
            y = jnp.dot(lhs, rows, preferred_element_type=F32)
            out_ref[t0 + i] = x_ref[t0 + i] + gate * (y[0:ROW_BF] + y[ROW_BF:])
        return carry

    lax.fori_loop(0, PEER_TT // GROUP, group, 0)


def _table_spec(n):
    return pl.BlockSpec((n, ROW_SUB, LANE), lambda i: (0, 0, 0), pipeline_mode=pl.Buffered(1))


def _idx_specs(tt):
    return [pl.BlockSpec((PEER_TOPK, tt), lambda i: (0, i), memory_space=pltpu.SMEM) for _ in range(PEER_HEADS)]


def peer_up(idxs, h, gate, tab_u):
    t, hk = gate.shape
    n = tab_u.shape[0]
    tt = PEER_TT
    return pl.pallas_call(
        _peer_up_kernel,
        grid=(t // tt,),
        in_specs=_idx_specs(tt) + [
                  pl.BlockSpec((tt, 8, LANE), lambda i: (i, 0, 0)),
                  pl.BlockSpec((tt, hk), lambda i: (i, 0)),
                  _table_spec(n)],
        out_specs=pl.BlockSpec((tt, hk), lambda i: (i, 0)),
        out_shape=jax.ShapeDtypeStruct((t, hk), F32),
        scratch_shapes=[pltpu.VMEM((hk * ROW_SUB, LANE), jnp.uint32), pltpu.VMEM((hk * ROW_SUB, LANE), jnp.uint32),
                        pltpu.VMEM((GROUP, hk * ROW_BF), F32)],
        compiler_params=_cparams(("arbitrary",), VMEM_LIMIT),
        name="peer_up",
    )(*idxs, h.reshape(t, 8, LANE), gate, tab_u)


def peer_down(idxs, act, x, gate2, tab_v, seq):
    t, hk = act.shape
    n = tab_v.shape[0]
    d = x.shape[1]
    tt = PEER_TT
    tiles_per_batch = seq // tt
    out = pl.pallas_call(
        _peer_down_kernel,
        grid=(t // tt,),
        in_specs=_idx_specs(tt) + [
                  pl.BlockSpec((tt, hk), lambda i: (i, 0)),
                  pl.BlockSpec((tt, 8, LANE), lambda i: (i, 0, 0)),
                  pl.BlockSpec((None, 8, LANE), lambda i: (i // tiles_per_batch, 0, 0)),
                  _table_spec(n)],
        out_specs=pl.BlockSpec((tt, 8, LANE), lambda i: (i, 0, 0)),
        out_shape=jax.ShapeDtypeStruct((t, 8, LANE), F32),
        scratch_shapes=[pltpu.VMEM((hk * ROW_SUB, LANE), jnp.uint32), pltpu.VMEM((hk * ROW_SUB, LANE), jnp.uint32)],
        compiler_params=_cparams(("arbitrary",), VMEM_LIMIT),
        name="peer_down",
    )(*idxs, act, x.reshape(t, 8, LANE), gate2.reshape(-1, 8, LANE), tab_v)
    return out.reshape(t, d)


def peer_ffn(x, g, sc, sh, gate2, wq, keys, tab_u, tab_v, seq):
    qry, h = norm_linear(x, g, sc, sh, wq, jnp.zeros((wq.shape[1],), F32), seq, emit_h=True)
    eid_t, gate_t = peer_route(qry, keys)
    idxs = [eid_t[hd * PEER_TOPK:(hd + 1) * PEER_TOPK] for hd in range(PEER_HEADS)]
    act = peer_up(idxs, h, gate_t.T, tab_u)
    return peer_down(idxs, act, x, gate2, tab_v, seq)


def _even_mixer(xt, g, sc, sh, gate, w_in, conv_w, igate_b, fgate_b, mnorm_g, qn_g, kn_g, w_out, bsz, seq):
    d = xt.shape[1]
    n_gate = 2 * M_HEADS
    g0 = 4 * M_W
    w_main = jnp.concatenate([w_in[:, :g0], w_in[:, g0 + n_gate:]], axis=1).astype(BF16)
    w_gate = jnp.pad(w_in[:, g0:g0 + n_gate], ((0, 0), (0, LANE - n_gate)))
    b_gate = jnp.pad(jnp.concatenate([igate_b, fgate_b]), (0, LANE - n_gate))
    z = norm_linear(xt, g, sc, sh, w_main, jnp.zeros((w_main.shape[1],), F32), seq)
    zg = norm_linear(xt, g, sc, sh, w_gate, b_gate, seq)
    icol, fcol, irow, frow = split_gates(zg)
    hm = mlstm_mixer(z, conv_w, icol, fcol, irow, frow, mnorm_g, bsz, seq)
    blk = g0 // LANE
    nblk = B_W // LANE
    qn, kn, kmean = head_norm(z, blk, blk + nblk, nblk, qn_g, kn_g, True)
    hb = moba_attention(qn, kn, kmean, z, blk + 2 * nblk, bsz, seq)
    w_out = w_out.astype(BF16)
    return linear_residual([hm, hb], [w_out[:M_W], w_out[M_W:]], xt, gate, seq)


def _odd_mixer(xt, g, sc, sh, gate, w_in, qn_g, kn_g, lam_p, onorm_g, w_out, lam_init, bsz, seq):
    z = norm_linear(xt, g, sc, sh, w_in.astype(BF16), jnp.zeros((w_in.shape[1],), F32), seq)
    nblk = 2 * C_HEADS * C_DH // LANE
    qn, kn = head_norm(z, 0, nblk, nblk, qn_g, kn_g, False)
    o = diff_attention(qn, kn, z, 2 * nblk, lam_p, onorm_g, lam_init, bsz, seq)
    return linear_residual([o], [w_out.astype(BF16)], xt, gate, seq)


def kernel(x, c, ada_w, ada_b, norm_mix_g, norm_ffn_g, ev_w_in, ev_conv_w, ev_igate_b, ev_fgate_b, ev_mnorm_g,
           ev_qn_g, ev_kn_g, ev_w_out, od_w_in, od_qn_g, od_kn_g, od_lam, od_onorm_g, od_w_out, peer_wq,
           peer_keys, peer_u, peer_v):
    bsz, seq, d = x.shape
    depth = ada_w.shape[0]
    mod = adaln_mod(c, ada_w, ada_b).reshape(depth, bsz, 6, 1, d)
    xt = x.reshape(bsz * seq, d)
    for layer in range(depth):
        sh1, sc1, g1, sh2, sc2, g2 = [mod[layer, :, i] for i in range(6)]
        if layer % 2 == 0:
            e = layer // 2
            xt = _even_mixer(xt, norm_mix_g[layer], sc1, sh1, g1, ev_w_in[e], ev_conv_w[e], ev_igate_b[e],
                             ev_fgate_b[e], ev_mnorm_g[e], ev_qn_g[e], ev_kn_g[e], ev_w_out[e], bsz, seq)
        else:
            o = layer // 2
            lam_init = 0.8 - 0.6 * math.exp(-0.3 * layer)
            xt = _odd_mixer(xt, norm_mix_g[layer], sc1, sh1, g1, od_w_in[o], od_qn_g[o], od_kn_g[o], od_lam[o],
                            od_onorm_g[o], od_w_out[o], lam_init, bsz, seq)
        xt = peer_ffn(xt, norm_ffn_g[layer], sc2, sh2, g2, peer_wq[layer].astype(BF16), peer_keys[layer],
                      pack_table(peer_u[layer]), pack_table(peer_v[layer]), seq)
    return xt.reshape(bsz, seq, d)
```

```python
import functools
import math

import jax
import jax.numpy as jnp
from jax import lax
from jax.experimental import pallas as pl
from jax.experimental.pallas import tpu as pltpu

F32 = jnp.float32
BF16 = jnp.bfloat16
HIGHEST = lax.Precision.HIGHEST
NEG_INF = float("-inf")

EPS = 1e-6
D_MODEL = 1024
M_HEADS, M_DH, M_CHUNK, CONV_K = 4, 128, 128, 4
M_W = M_HEADS * M_DH
B_HEADS, B_DH, MOBA_BLOCK, MOBA_TOPK = 8, 64, 256, 3
B_W = B_HEADS * B_DH
C_HEADS, C_DH, C_VDH = 8, 64, 128
PEER_HEADS, PEER_NKEYS, PEER_DK, PEER_TOPK = 8, 128, 256, 16

LANE = 128
VMEM_LIMIT = 56 * 1024 * 1024


def _cparams(sem, vmem=None):
    return pltpu.CompilerParams(dimension_semantics=sem, vmem_limit_bytes=vmem)


def _mod_kernel(c_ref, w_ref, b_ref, o_ref):
    c = c_ref[...]
    ca = c * (1.0 / (1.0 + jnp.exp(-c)))
    o_ref[...] = jnp.dot(ca, w_ref[...], precision=HIGHEST, preferred_element_type=F32) + b_ref[...]


def adaln_mod(c, ada_w, ada_b):
    depth, d, n = ada_w.shape
    bsz = c.shape[0]
    tn = 512
    return pl.pallas_call(
        _mod_kernel,
        grid=(depth, n // tn),
        in_specs=[pl.BlockSpec((bsz, d), lambda l, j: (0, 0)),
                  pl.BlockSpec((None, d, tn), lambda l, j: (l, 0, j)),
                  pl.BlockSpec((None, 1, tn), lambda l, j: (l, 0, j))],
        out_specs=pl.BlockSpec((None, bsz, tn), lambda l, j: (l, 0, j)),
        out_shape=jax.ShapeDtypeStruct((depth, bsz, n), F32),
        compiler_params=_cparams(("parallel", "parallel")),
        name="adaln_mod",
    )(c, ada_w, ada_b.reshape(depth, 1, n))


def _norm_linear_kernel(x_ref, g_ref, sc_ref, sh_ref, w_ref, b_ref, o_ref, *rest, emit_h, precise):
    if emit_h:
        h_out_ref, h_sc = rest
    else:
        (h_sc,) = rest

    @pl.when(pl.program_id(1) == 0)
    def _():
        x = x_ref[...]
        y = x * lax.rsqrt(jnp.mean(x * x, axis=-1, keepdims=True) + EPS) * g_ref[...]
        h = y * (1.0 + sc_ref[...]) + sh_ref[...]
        h_sc[...] = h.astype(h_sc.dtype)
        if emit_h:
            h_out_ref[...] = h

    if precise:
        acc = jnp.dot(h_sc[...], w_ref[...], precision=HIGHEST, preferred_element_type=F32)
    else:
        acc = jnp.dot(h_sc[...], w_ref[...], preferred_element_type=F32)
    o_ref[...] = (acc + b_ref[...]).astype(o_ref.dtype)


def norm_linear(x, g, sc, sh, w, bias, seq, *, tm=1024, tn=512, emit_h=False, out_dtype=F32):
    t, d = x.shape
    n = w.shape[1]
    tn = min(tn, n)
    tm = min(tm, seq)
    precise = w.dtype == F32
    rows_per_batch = seq // tm
    out_shape = [jax.ShapeDtypeStruct((t, n), out_dtype)]
    out_specs = [pl.BlockSpec((tm, tn), lambda i, j: (i, j))]
    if emit_h:
        out_shape.append(jax.ShapeDtypeStruct((t, d), F32))
        out_specs.append(pl.BlockSpec((tm, d), lambda i, j: (i, 0)))
    res = pl.pallas_call(
        functools.partial(_norm_linear_kernel, emit_h=emit_h, precise=precise),
        grid=(t // tm, n // tn),
        in_specs=[pl.BlockSpec((tm, d), lambda i, j: (i, 0)),
                  pl.BlockSpec((1, d), lambda i, j: (0, 0)),
                  pl.BlockSpec((None, 1, d), lambda i, j: (i // rows_per_batch, 0, 0)),
                  pl.BlockSpec((None, 1, d), lambda i, j: (i // rows_per_batch, 0, 0)),
                  pl.BlockSpec((d, tn), lambda i, j: (0, j)),
                  pl.BlockSpec((1, tn), lambda i, j: (0, j))],
        out_specs=out_specs,
        out_shape=out_shape,
        scratch_shapes=[pltpu.VMEM((tm, d), w.dtype)],
        compiler_params=_cparams(("parallel", "arbitrary"), VMEM_LIMIT),
        name="norm_linear",
    )(x, g.reshape(1, d), sc, sh, w, bias.reshape(1, n))
    return res if emit_h else res[0]


def _linear_residual_kernel(*refs, nparts):
    y_refs, w_refs = refs[:nparts], refs[nparts:2 * nparts]
    x_ref, gate_ref, o_ref = refs[2 * nparts:]
    acc = None
    for y_ref, w_ref in zip(y_refs, w_refs):
        part = jnp.dot(y_ref[...].astype(BF16), w_ref[...], preferred_element_type=F32)
        acc = part if acc is None else acc + part
    o_ref[...] = x_ref[...] + gate_ref[...] * acc


def linear_residual(ys, ws, x, gate, seq, *, tm=512):
    t, d = x.shape
    rows_per_batch = seq // tm
    return pl.pallas_call(
        functools.partial(_linear_residual_kernel, nparts=len(ys)),
        grid=(t // tm,),
        in_specs=([pl.BlockSpec((tm, y.shape[1]), lambda i: (i, 0)) for y in ys]
                  + [pl.BlockSpec(w.shape, lambda i: (0, 0)) for w in ws]
                  + [pl.BlockSpec((tm, d), lambda i: (i, 0)),
                     pl.BlockSpec((None, 1, d), lambda i: (i // rows_per_batch, 0, 0))]),
        out_specs=pl.BlockSpec((tm, d), lambda i: (i, 0)),
        out_shape=jax.ShapeDtypeStruct((t, d), F32),
        compiler_params=_cparams(("parallel",), VMEM_LIMIT),
        name="linear_residual",
    )(*ys, *ws, x, gate)


def _shift_rows(x, tail, s):
    if s == 0:
        return x
    xs = pltpu.roll(x, s, axis=0)
    ts = pltpu.roll(tail, s, axis=0)
    row = lax.broadcasted_iota(jnp.int32, (8, x.shape[1]), 0)
    top = jnp.where(row < s, ts, xs[0:8])
    return jnp.concatenate([top, xs[8:]], axis=0)


def _conv_silu(x, tail, w):
    acc = None
    for j in range(CONV_K):
        term = w[j:j + 1, :] * _shift_rows(x, tail, CONV_K - 1 - j)
        acc = term if acc is None else acc + term
    return acc * (1.0 / (1.0 + jnp.exp(-acc)))


def _log_sigmoid(x):
    return jnp.minimum(x, 0.0) - jnp.log(1.0 + jnp.exp(-jnp.abs(x)))


def _mlstm_kernel(q_ref, k_ref, v_ref, o_ref, wq_ref, wk_ref, icol_ref, fcol_ref, irow_ref, frow_ref,
                  g_ref, out_ref, c_sc, n_sc, m_sc, qt_sc, kt_sc):
    L = M_CHUNK

    @pl.when(pl.program_id(2) == 0)
    def _():
        c_sc[...] = jnp.zeros_like(c_sc)
        n_sc[...] = jnp.zeros_like(n_sc)
        m_sc[...] = jnp.zeros_like(m_sc)
        qt_sc[...] = jnp.zeros_like(qt_sc)
        kt_sc[...] = jnp.zeros_like(kt_sc)

    q_raw = q_ref[...]
    k_raw = k_ref[...]
    q = _conv_silu(q_raw, qt_sc[...], wq_ref[...])
    k = _conv_silu(k_raw, kt_sc[...], wk_ref[...]) * (M_DH ** -0.5)
    qt_sc[...] = q_raw[L - 8:, :]
    kt_sc[...] = k_raw[L - 8:, :]
    v = v_ref[...]

    row = lax.broadcasted_iota(jnp.int32, (L, L), 0)
    col = lax.broadcasted_iota(jnp.int32, (L, L), 1)
    causal = row >= col
    tri = jnp.where(causal, 1.0, 0.0).astype(F32)
    triu = jnp.where(row <= col, 1.0, 0.0).astype(F32)

    lf_col = _log_sigmoid(fcol_ref[...])
    lf_row = _log_sigmoid(frow_ref[...])
    ig_col = icol_ref[...]
    ig_row = irow_ref[...]
    b_col = jnp.dot(tri, jnp.broadcast_to(lf_col, (L, L)), precision=HIGHEST,
                    preferred_element_type=F32)[:, 0:1]
    b_row = jnp.dot(jnp.broadcast_to(lf_row, (8, L)), triu, precision=HIGHEST,
                    preferred_element_type=F32)[0:1, :]
    b_end = b_row[:, L - 1:L]

    m_prev = m_sc[...]
    c_prev = c_sc[...]
    n_prev = n_sc[...]

    log_d = jnp.where(causal, b_col - b_row + ig_row, NEG_INF)
    a_inter = b_col + m_prev
    m_t = jnp.maximum(a_inter, jnp.max(log_d, axis=1, keepdims=True))
    w_inter = jnp.exp(a_inter - m_t)
    qb = q.astype(BF16)
    kb = k.astype(BF16)
    vb = v.astype(BF16)
    s = lax.dot_general(qb, kb, (((1,), (1,)), ((), ())), preferred_element_type=F32)
    s = s * jnp.exp(log_d - m_t)
    num = (w_inter * jnp.dot(qb, c_prev.astype(BF16), preferred_element_type=F32)
           + jnp.dot(s.astype(BF16), vb, preferred_element_type=F32))
    den = (w_inter * jnp.sum(q * n_prev, axis=1, keepdims=True) + jnp.sum(s, axis=1, keepdims=True))
    h = num / jnp.maximum(jnp.abs(den), jnp.exp(-m_t))

    w_end_row = b_end - b_row + ig_row
    m_loc = jnp.max(w_end_row, axis=1, keepdims=True)
    e_col = jnp.exp(b_end - b_col + ig_col - m_loc)
    ke = k * e_col
    c_loc = lax.dot_general(ke.astype(BF16), vb, (((0,), (0,)), ((), ())), preferred_element_type=F32)
    n_loc = jnp.sum(ke, axis=0, keepdims=True)
    m_new = jnp.maximum(b_end + m_prev, m_loc)
    a = jnp.exp(b_end + m_prev - m_new)
    sc = jnp.exp(m_loc - m_new)
    c_sc[...] = a * c_prev + sc * c_loc
    n_sc[...] = a * n_prev + sc * n_loc
    m_sc[...] = m_new

    hn = h * lax.rsqrt(jnp.mean(h * h, axis=1, keepdims=True) + EPS) * g_ref[...]
    og = o_ref[...]
    out_ref[...] = (hn * (1.0 / (1.0 + jnp.exp(-og)))).astype(out_ref.dtype)


def mlstm_mixer(z, conv_w, icol, fcol, irow, frow, mnorm_g, bsz, seq):
    t = bsz * seq
    L = M_CHUNK
    nc = seq // L
    hq = M_W // LANE

    def zspec(sec):
        return pl.BlockSpec((L, M_DH), lambda b, h, c, sec=sec: (b * nc + c, sec * hq + h))

    return pl.pallas_call(
        _mlstm_kernel,
        grid=(bsz, M_HEADS, nc),
        in_specs=[zspec(0), zspec(1), zspec(2), zspec(3),
                  pl.BlockSpec((CONV_K, M_DH), lambda b, h, c: (0, h)),
                  pl.BlockSpec((CONV_K, M_DH), lambda b, h, c: (0, hq + h)),
                  pl.BlockSpec((None, L, 1), lambda b, h, c: (h, b * nc + c, 0)),
                  pl.BlockSpec((None, L, 1), lambda b, h, c: (h, b * nc + c, 0)),
                  pl.BlockSpec((None, None, 1, L), lambda b, h, c: (h, b * nc + c, 0, 0)),
                  pl.BlockSpec((None, None, 1, L), lambda b, h, c: (h, b * nc + c, 0, 0)),
                  pl.BlockSpec((None, 1, M_DH), lambda b, h, c: (h, 0, 0))],
        out_specs=pl.BlockSpec((L, M_DH), lambda b, h, c: (b * nc + c, h)),
        out_shape=jax.ShapeDtypeStruct((t, M_W), F32),
        scratch_shapes=[pltpu.VMEM((M_DH, M_DH), F32), pltpu.VMEM((1, M_DH), F32), pltpu.VMEM((1, 1), F32),
                        pltpu.VMEM((8, M_DH), F32), pltpu.VMEM((8, M_DH), F32)],
        compiler_params=_cparams(("parallel", "parallel", "arbitrary")),
        name="mlstm",
    )(z, z, z, z, conv_w, conv_w, icol, fcol, irow, frow, mnorm_g.reshape(M_HEADS, 1, M_DH))


def split_gates(zg):
    t = zg.shape[0]
    gt = zg[:, :2 * M_HEADS].T
    cols = gt.reshape(2 * M_HEADS, t, 1)
    rows = gt.reshape(2 * M_HEADS, t // M_CHUNK, 1, M_CHUNK)
    return cols[:M_HEADS], cols[M_HEADS:], rows[:M_HEADS], rows[M_HEADS:]


ATT_BLOCK = 256
ATT_UNROLL = 4
HALF = 64


def _half_rms_norm(x, g):
    lo = lax.broadcasted_iota(jnp.int32, x.shape, 1) < HALF
    x2 = x * x
    s_lo = jnp.sum(jnp.where(lo, x2, 0.0), axis=1, keepdims=True)
    s_hi = jnp.sum(jnp.where(lo, 0.0, x2), axis=1, keepdims=True)
    r = jnp.where(lo, lax.rsqrt(s_lo * (1.0 / HALF) + EPS), lax.rsqrt(s_hi * (1.0 / HALF) + EPS))
    return x * r * g


def _head_norm_kernel(q_ref, k_ref, gq_ref, gk_ref, qn_ref, kn_ref, *rest):
    for c in range(q_ref.shape[1] // LANE):
        cols = slice(c * LANE, (c + 1) * LANE)
        qn_ref[:, cols] = _half_rms_norm(q_ref[:, cols], gq_ref[...])
        kn = _half_rms_norm(k_ref[:, cols], gk_ref[...])
        kn_ref[:, cols] = kn.astype(kn_ref.dtype)
        if rest:
            rest[0][:, cols] = jnp.mean(kn, axis=0, keepdims=True)


def head_norm(z, q_blk0, k_blk0, nblk, gq, gk, with_kmean):
    t = z.shape[0]
    w = nblk * LANE
    tb = ATT_BLOCK
    gq2 = jnp.concatenate([gq, gq]).reshape(1, LANE)
    gk2 = jnp.concatenate([gk, gk]).reshape(1, LANE)
    assert q_blk0 % nblk == 0 and k_blk0 % nblk == 0
    out_shape = [jax.ShapeDtypeStruct((t, w), F32), jax.ShapeDtypeStruct((t, w), BF16)]
    out_specs = [pl.BlockSpec((tb, w), lambda i: (i, 0)), pl.BlockSpec((tb, w), lambda i: (i, 0))]
    if with_kmean:
        out_shape.append(jax.ShapeDtypeStruct((t // tb, 1, w), F32))
        out_specs.append(pl.BlockSpec((None, 1, w), lambda i: (i, 0, 0)))
    return pl.pallas_call(
        _head_norm_kernel,
        grid=(t // tb,),
        in_specs=[pl.BlockSpec((tb, w), lambda i: (i, q_blk0 // nblk)),
                  pl.BlockSpec((tb, w), lambda i: (i, k_blk0 // nblk)),
                  pl.BlockSpec((1, LANE), lambda i: (0, 0)),
                  pl.BlockSpec((1, LANE), lambda i: (0, 0))],
        out_specs=out_specs,
        out_shape=out_shape,
        compiler_params=_cparams(("parallel",)),
        name="head_norm",
    )(z, z, gq2, gk2)


def _attn_kernel(*refs, moba, lam_init):
    if moba:
        (slope_ref, q_ref, k_ref, vt_ref, km_ref, out_ref, sel_sc, m_sc, l_sc, acc_sc) = refs
    else:
        (slope_ref, q_ref, k_ref, vt_ref, lam_ref, g_ref, out_ref, m_sc, l_sc, acc_sc) = refs
    tb = ATT_BLOCK
    grp = pl.program_id(1)
    j = pl.program_id(2)
    scale = HALF ** -0.5

    q = q_ref[...]
    lo = lax.broadcasted_iota(jnp.int32, (tb, LANE), 1) < HALF
    qs = q * scale
    q_half = [jnp.where(lo, qs, 0.0).astype(BF16), jnp.where(lo, 0.0, qs).astype(BF16)]
    krow = lax.broadcasted_iota(jnp.int32, (tb, tb), 0)
    causal = krow <= lax.broadcasted_iota(jnp.int32, (tb, tb), 1)
    if moba:
        slopes = [slope_ref[2 * grp], slope_ref[2 * grp + 1]]
    else:
        slopes = [slope_ref[grp], slope_ref[grp]]
    kpos = krow.astype(F32)
    bias = [slopes[0] * kpos, slopes[1] * kpos] if moba else [slopes[0] * kpos] * 2

    if moba:
        km = km_ref[...]
        blk = lax.broadcasted_iota(jnp.int32, (LANE, tb), 0)
        for hf in range(2):
            qm = jnp.where(lo, q, 0.0) if hf == 0 else jnp.where(lo, 0.0, q)
            gate = lax.dot_general(km, qm, (((1,), (1,)), ((), ())), precision=HIGHEST,
                                   preferred_element_type=F32)
            gate = jnp.where(blk < j, gate, NEG_INF)
            sel = jnp.zeros((LANE, tb), F32)
            for _ in range(MOBA_TOPK):
                mx = jnp.max(gate, axis=0, keepdims=True)
                idx = jnp.min(jnp.where(gate == mx, blk, LANE), axis=0, keepdims=True)
                hit = blk == idx
                sel = jnp.where(jnp.logical_and(hit, mx > NEG_INF), 1.0, sel)
                gate = jnp.where(hit, NEG_INF, gate)
            sel_sc[hf] = sel

    m_sc[...] = jnp.full(m_sc.shape, NEG_INF, F32)
    l_sc[...] = jnp.zeros(l_sc.shape, F32)
    acc_sc[...] = jnp.zeros(acc_sc.shape, F32)

    def block_update(blocks, diag):
        m_old = [m_sc[0], m_sc[1]]
        l_old = [l_sc[0], l_sc[1]]
        acc_old = [acc_sc[0], acc_sc[1]]
        xs, shifts, vts = [], [], []
        for n in blocks:
            start = pl.multiple_of(n * tb, tb)
            kb = k_ref[pl.ds(start, tb), :]
            vts.append(vt_ref[n])
            off = ((n - j) * tb).astype(F32)
            for hf in range(2):
                x = lax.dot_general(kb, q_half[hf], (((1,), (1,)), ((), ())), preferred_element_type=F32)
                x = x + bias[hf]
                if diag:
                    x = jnp.where(causal, x, NEG_INF)
                elif moba:
                    x = jnp.where(sel_sc[hf, pl.ds(n, 1), :] > 0.5, x, NEG_INF)
                xs.append(x)
                shifts.append(slopes[hf] * off)
        for hf in range(2):
            ids = [2 * b + hf for b in range(len(blocks))]
            m_new = m_old[hf]
            for i in ids:
                m_new = jnp.maximum(m_new, jnp.max(xs[i], axis=0, keepdims=True) + shifts[i])
            alpha = jnp.exp(m_old[hf] - m_new)
            l_new = alpha * l_old[hf]
            acc_new = alpha * acc_old[hf]
            for b, i in enumerate(ids):
                p = jnp.exp(xs[i] - (m_new - shifts[i]))
                l_new = l_new + jnp.sum(p, axis=0, keepdims=True)
                acc_new = acc_new + jnp.dot(vts[b], p.astype(BF16), preferred_element_type=F32)
            l_sc[hf] = l_new
            acc_sc[hf] = acc_new
            m_sc[hf] = m_new

    block_update([j], True)

    def body(i, carry):
        block_update([ATT_UNROLL * i + u for u in range(ATT_UNROLL)], False)
        return carry

    lax.fori_loop(0, j // ATT_UNROLL, body, 0)

    done = (j // ATT_UNROLL) * ATT_UNROLL
    width = ATT_UNROLL // 2
    while width >= 1:
        has = (j & width) != 0

        @pl.when(has)
        def _(done=done, width=width):
            block_update([done + u for u in range(width)], False)

        done = done + jnp.where(has, width, 0)
        width //= 2

    o0 = acc_sc[0] * (1.0 / l_sc[0])
    o1 = acc_sc[1] * (1.0 / l_sc[1])
    if moba:
        first_head = lax.broadcasted_iota(jnp.int32, (LANE, tb), 0) < HALF
        out_ref[...] = jnp.where(first_head, o0, o1).T.astype(out_ref.dtype)
    else:
        lp = lam_ref[...]
        s1 = jnp.sum(lp[0:1] * lp[1:2], axis=1, keepdims=True)
        s2 = jnp.sum(lp[2:3] * lp[3:4], axis=1, keepdims=True)
        lam = jnp.exp(s1) - jnp.exp(s2) + lam_init
        o = (o0 - lam * o1).T
        o = o * lax.rsqrt(jnp.mean(o * o, axis=1, keepdims=True) + EPS) * g_ref[...]
        out_ref[...] = (o * (1.0 - lam_init)).astype(out_ref.dtype)


def _alibi_slopes(n_heads):
    return jnp.asarray([2.0 ** (-8.0 * (h + 1) / n_heads) for h in range(n_heads)], dtype=F32)


def _value_blocks_t(z, v_blk0, ngrp, bsz, seq):
    nb = seq // ATT_BLOCK
    v = z[:, v_blk0 * LANE:(v_blk0 + ngrp) * LANE].astype(BF16)
    return v.reshape(bsz, nb, ATT_BLOCK, ngrp, LANE).transpose(0, 3, 1, 4, 2)


def _attn_scratch(tb):
    return [pltpu.VMEM((2, 1, tb), F32), pltpu.VMEM((2, 1, tb), F32), pltpu.VMEM((2, LANE, tb), F32)]


def moba_attention(qn, kn, kmean, z, v_blk0, bsz, seq):
    t = bsz * seq
    tb = ATT_BLOCK
    nb = seq // tb
    ngrp = B_W // LANE
    km = jnp.pad(kmean.reshape(bsz, nb, B_W), ((0, 0), (0, LANE - nb), (0, 0)))
    return pl.pallas_call(
        functools.partial(_attn_kernel, moba=True, lam_init=None),
        grid=(bsz, ngrp, nb),
        in_specs=[pl.BlockSpec(memory_space=pltpu.SMEM),
                  pl.BlockSpec((tb, LANE), lambda b, g, j: (b * nb + j, g)),
                  pl.BlockSpec((seq, LANE), lambda b, g, j: (b, g)),
                  pl.BlockSpec((None, None, nb, LANE, tb), lambda b, g, j: (b, g, 0, 0, 0)),
                  pl.BlockSpec((None, LANE, LANE), lambda b, g, j: (b, 0, g))],
        out_specs=pl.BlockSpec((tb, LANE), lambda b, g, j: (b * nb + j, g)),
        out_shape=jax.ShapeDtypeStruct((t, B_W), F32),
        scratch_shapes=[pltpu.VMEM((2, LANE, tb), F32)] + _attn_scratch(tb),
        compiler_params=_cparams(("parallel", "parallel", "arbitrary"), VMEM_LIMIT),
        name="moba_attention",
    )(_alibi_slopes(B_HEADS), qn, kn, _value_blocks_t(z, v_blk0, ngrp, bsz, seq), km)


def diff_attention(qn, kn, z, v_blk0, lam_p, onorm_g, lam_init, bsz, seq):
    t = bsz * seq
    tb = ATT_BLOCK
    nb = seq // tb
    return pl.pallas_call(
        functools.partial(_attn_kernel, moba=False, lam_init=lam_init),
        grid=(bsz, C_HEADS, nb),
        in_specs=[pl.BlockSpec(memory_space=pltpu.SMEM),
                  pl.BlockSpec((tb, LANE), lambda b, g, j: (b * nb + j, g)),
                  pl.BlockSpec((seq, LANE), lambda b, g, j: (b, g)),
                  pl.BlockSpec((None, None, nb, LANE, tb), lambda b, g, j: (b, g, 0, 0, 0)),
                  pl.BlockSpec((4, C_DH), lambda b, g, j: (0, 0)),
                  pl.BlockSpec((1, C_VDH), lambda b, g, j: (0, 0))],
        out_specs=pl.BlockSpec((tb, LANE), lambda b, g, j: (b * nb + j, g)),
        out_shape=jax.ShapeDtypeStruct((t, C_HEADS * C_VDH), F32),
        scratch_shapes=_attn_scratch(tb),
        compiler_params=_cparams(("parallel", "parallel", "arbitrary"), VMEM_LIMIT),
        name="diff_attention",
    )(_alibi_slopes(C_HEADS), qn, kn, _value_blocks_t(z, v_blk0, C_HEADS, bsz, seq), lam_p,
      onorm_g.reshape(1, C_VDH))


def _top_rows(s, count):
    n = s.shape[0]
    rows = lax.broadcasted_iota(jnp.int32, s.shape, 0)
    vals, idxs = [], []
    for _ in range(count):
        mx = jnp.max(s, axis=0, keepdims=True)
        ix = jnp.min(jnp.where(s == mx, rows, n), axis=0, keepdims=True)
        vals.append(mx)
        idxs.append(ix)
        s = jnp.where(rows == ix, NEG_INF, s)
    return jnp.concatenate(vals, axis=0), jnp.concatenate(idxs, axis=0)


def _peer_route_kernel(q_ref, keys_ref, eid_ref, gate_ref):
    kk = PEER_TOPK
    half = PEER_DK // 2
    tops = []
    for p in range(2):
        qp = q_ref[:, p * half:(p + 1) * half]
        st = lax.dot_general(keys_ref[p], qp, (((1,), (1,)), ((), ())), precision=HIGHEST,
                             preferred_element_type=F32)
        tops.append(_top_rows(st, kk))
    (s0, i0), (s1, i1) = tops
    widths = [kk // (a + 1) for a in range(kk)]
    pad = -sum(widths) % 8
    cand_s = jnp.concatenate([s0[a:a + 1] + s1[0:widths[a]] for a in range(kk)]
                             + [jnp.full((pad, s0.shape[1]), NEG_INF, F32)], axis=0)
    cand_i = jnp.concatenate([i0[a:a + 1] * PEER_NKEYS + i1[0:widths[a]] for a in range(kk)]
                             + [jnp.zeros((pad, s0.shape[1]), jnp.int32)], axis=0)
    score, pos = _top_rows(cand_s, kk)
    rows = lax.broadcasted_iota(jnp.int32, cand_i.shape, 0)
    eid = jnp.concatenate(
        [jnp.sum(jnp.where(rows == pos[a:a + 1], cand_i, 0), axis=0, keepdims=True) for a in range(kk)], axis=0)
    e = jnp.exp(score - score[0:1])
    eid_ref[...] = eid
    gate_ref[...] = e / jnp.sum(e, axis=0, keepdims=True)


def peer_route(qry, keys, *, tt=256):
    t = qry.shape[0]
    hk = PEER_HEADS * PEER_TOPK
    return pl.pallas_call(
        _peer_route_kernel,
        grid=(t // tt, PEER_HEADS),
        in_specs=[pl.BlockSpec((tt, PEER_DK), lambda i, h: (i, h)),
                  pl.BlockSpec((2, PEER_NKEYS, PEER_DK // 2), lambda i, h: (0, 0, 0))],
        out_specs=[pl.BlockSpec((PEER_TOPK, tt), lambda i, h: (h, i)),
                   pl.BlockSpec((PEER_TOPK, tt), lambda i, h: (h, i))],
        out_shape=[jax.ShapeDtypeStruct((hk, t), jnp.int32), jax.ShapeDtypeStruct((hk, t), F32)],
        compiler_params=_cparams(("parallel", "parallel")),
        name="peer_route",
    )(qry, keys)


ROW_SUB = 4
PEER_TT = 128


PEER_HK = PEER_HEADS * PEER_TOPK
ROW_BF = 2 * ROW_SUB
GROUP = 8
NSLOT = 2


def pack_table(tab):
    n, d = tab.shape
    b = lax.bitcast_convert_type(tab.astype(BF16), jnp.uint16).astype(jnp.uint32)
    b = b.reshape(n, ROW_SUB, 2, LANE)
    return b[:, :, 0, :] | (b[:, :, 1, :] << 16)


def _split_bf16(x):
    hi = x.astype(BF16)
    lo = (x - hi.astype(F32)).astype(BF16)
    return jnp.concatenate([hi, lo], axis=0)


def _gelu(x):
    return 0.5 * x * (1.0 + lax.erf(x * (2.0 ** -0.5)))


def _gather_rows(idx_refs, tab_ref, dst_ref, t):
    for k in range(PEER_HK):
        dst_ref[k * ROW_SUB:(k + 1) * ROW_SUB, :] = tab_ref[idx_refs[k // PEER_TOPK][k % PEER_TOPK, t]]


def _next_token(t0, i):
    assert GROUP % NSLOT == 0
    if i + 1 < GROUP:
        return t0 + i + 1
    return jnp.minimum(t0 + GROUP, PEER_TT - 1)


def _diag_mask(rows):
    shape = (rows, PEER_HK * ROW_BF)
    return (lax.broadcasted_iota(jnp.int32, shape, 1) % ROW_BF) == (lax.broadcasted_iota(jnp.int32, shape, 0) % ROW_BF)


def _peer_up_kernel(*refs):
    idx_refs = refs[:PEER_HEADS]
    h_ref, gate_ref, tab_ref, act_ref = refs[PEER_HEADS:PEER_HEADS + 4]
    slots = refs[PEER_HEADS + 4:PEER_HEADS + 4 + NSLOT]
    c_sc = refs[PEER_HEADS + 4 + NSLOT]
    wide = PEER_HK * ROW_BF
    mask = _diag_mask(2 * ROW_BF)
    fold = (lax.broadcasted_iota(jnp.int32, (wide, PEER_HK), 0) // ROW_BF
            == lax.broadcasted_iota(jnp.int32, (wide, PEER_HK), 1)).astype(BF16)

    def group(g, carry):
        t0 = g * GROUP
        for i in range(GROUP):
            _gather_rows(idx_refs, tab_ref, slots[(i + 1) % NSLOT], _next_token(t0, i))
            rows = pltpu.bitcast(slots[i % NSLOT][...], BF16)
            xs = _split_bf16(h_ref[t0 + i])
            y = lax.dot_general(xs, rows, (((1,), (1,)), ((), ())), preferred_element_type=F32)
            y = jnp.where(mask, y, 0.0)
            c_sc[i:i + 1, :] = jnp.sum(y, axis=0, keepdims=True)
        s = jnp.dot(_split_bf16(c_sc[...]), fold, preferred_element_type=F32)
        s = s[0:GROUP] + s[GROUP:]
        rows8 = pl.ds(pl.multiple_of(t0, GROUP), GROUP)
        act_ref[rows8, :] = _gelu(s) * gate_ref[rows8, :]
        return carry

    _gather_rows(idx_refs, tab_ref, slots[0], 0)
    lax.fori_loop(0, PEER_TT // GROUP, group, 0)


def _peer_down_kernel(*refs):
    idx_refs = refs[:PEER_HEADS]
    act_ref, x_ref, g_ref, tab_ref, out_ref = refs[PEER_HEADS:PEER_HEADS + 5]
    slots = refs[PEER_HEADS + 5:PEER_HEADS + 5 + NSLOT]
    wide = PEER_HK * ROW_BF
    mask = _diag_mask(ROW_BF)
    spread = (lax.broadcasted_iota(jnp.int32, (PEER_HK, wide), 1) // ROW_BF
              == lax.broadcasted_iota(jnp.int32, (PEER_HK, wide), 0)).astype(BF16)
    gate = g_ref[...]

    def group(g, carry):
        t0 = g * GROUP
        acts = act_ref[pl.ds(pl.multiple_of(t0, GROUP), GROUP), :]
        a_wide = jnp.dot(_split_bf16(acts), spread, preferred_element_type=F32)
        for i in range(GROUP):
            _gather_rows(idx_refs, tab_ref, slots[(i + 1) % NSLOT], _next_token(t0, i))
            rows = pltpu.bitcast(slots[i % NSLOT][...], BF16)
            a_hi = jnp.where(mask, jnp.broadcast_to(a_wide[i:i + 1], (ROW_BF, wide)), 0.0)
            a_lo = jnp.where(mask, jnp.broadcast_to(a_wide[GROUP + i:GROUP + i + 1], (ROW_BF, wide)), 0.0)
            lhs = jnp.concatenate([a_hi, a_lo], axis=0).astype(BF16)
            y = jnp.dot(lhs, rows, preferred_element_type=F32)
            out_ref[t0 + i] = x_ref[t0 + i] + gate * (y[0:ROW_BF] + y[ROW_BF:])
        return carry

    _gather_rows(idx_refs, tab_ref, slots[0], 0)
    lax.fori_loop(0, PEER_TT // GROUP, group, 0)


def _table_spec(n):
    return pl.BlockSpec((n, ROW_SUB, LANE), lambda i: (0, 0, 0), pipeline_mode=pl.Buffered(1))


def _idx_specs(tt):
    return [pl.BlockSpec((PEER_TOPK, tt), lambda i: (0, i), memory_space=pltpu.SMEM) for _ in range(PEER_HEADS)]


def peer_up(idxs, h, gate, tab_u):
    t, hk = gate.shape
    n = tab_u.shape[0]
    tt = PEER_TT
    return pl.pallas_call(
        _peer_up_kernel,
        grid=(t // tt,),
        in_specs=_idx_specs(tt) + [
                  pl.BlockSpec((tt, 8, LANE), lambda i: (i, 0, 0)),
                  pl.BlockSpec((tt, hk), lambda i: (i, 0)),
                  _table_spec(n)],
        out_specs=pl.BlockSpec((tt, hk), lambda i: (i, 0)),
        out_shape=jax.ShapeDtypeStruct((t, hk), F32),
        scratch_shapes=[pltpu.VMEM((hk * ROW_SUB, LANE), jnp.uint32) for _ in range(NSLOT)]
                       + [pltpu.VMEM((GROUP, hk * ROW_BF), F32)],
        compiler_params=_cparams(("arbitrary",), VMEM_LIMIT),
        name="peer_up",
    )(*idxs, h.reshape(t, 8, LANE), gate, tab_u)


def peer_down(idxs, act, x, gate2, tab_v, seq):
    t, hk = act.shape
    n = tab_v.shape[0]
    d = x.shape[1]
    tt = PEER_TT
    tiles_per_batch = seq // tt
    out = pl.pallas_call(
        _peer_down_kernel,
        grid=(t // tt,),
        in_specs=_idx_specs(tt) + [
                  pl.BlockSpec((tt, hk), lambda i: (i, 0)),
                  pl.BlockSpec((tt, 8, LANE), lambda i: (i, 0, 0)),
                  pl.BlockSpec((None, 8, LANE), lambda i: (i // tiles_per_batch, 0, 0)),
                  _table_spec(n)],
        out_specs=pl.BlockSpec((tt, 8, LANE), lambda i: (i, 0, 0)),
        out_shape=jax.ShapeDtypeStruct((t, 8, LANE), F32),
        scratch_shapes=[pltpu.VMEM((hk * ROW_SUB, LANE), jnp.uint32) for _ in range(NSLOT)],
        compiler_params=_cparams(("arbitrary",), VMEM_LIMIT),
        name="peer_down",
    )(*idxs, act, x.reshape(t, 8, LANE), gate2.reshape(-1, 8, LANE), tab_v)
    return out.reshape(t, d)


def peer_ffn(x, g, sc, sh, gate2, wq, keys, tab_u, tab_v, seq):
    qry, h = norm_linear(x, g, sc, sh, wq, jnp.zeros((wq.shape[1],), F32), seq, emit_h=True)
    eid_t, gate_t = peer_route(qry, keys)
    idxs = [eid_t[hd * PEER_TOPK:(hd + 1) * PEER_TOPK] for hd in range(PEER_HEADS)]
    act = peer_up(idxs, h, gate_t.T, tab_u)
    return peer_down(idxs, act, x, gate2, tab_v, seq)


def _even_mixer(xt, g, sc, sh, gate, w_in, conv_w, igate_b, fgate_b, mnorm_g, qn_g, kn_g, w_out, bsz, seq):
    d = xt.shape[1]
    n_gate = 2 * M_HEADS
    g0 = 4 * M_W
    w_main = jnp.concatenate([w_in[:, :g0], w_in[:, g0 + n_gate:]], axis=1).astype(BF16)
    w_gate = jnp.pad(w_in[:, g0:g0 + n_gate], ((0, 0), (0, LANE - n_gate)))
    b_gate = jnp.pad(jnp.concatenate([igate_b, fgate_b]), (0, LANE - n_gate))
    z = norm_linear(xt, g, sc, sh, w_main, jnp.zeros((w_main.shape[1],), F32), seq)
    zg = norm_linear(xt, g, sc, sh, w_gate, b_gate, seq)
    icol, fcol, irow, frow = split_gates(zg)
    hm = mlstm_mixer(z, conv_w, icol, fcol, irow, frow, mnorm_g, bsz, seq)
    blk = g0 // LANE
    nblk = B_W // LANE
    qn, kn, kmean = head_norm(z, blk, blk + nblk, nblk, qn_g, kn_g, True)
    hb = moba_attention(qn, kn, kmean, z, blk + 2 * nblk, bsz, seq)
    w_out = w_out.astype(BF16)
    return linear_residual([hm, hb], [w_out[:M_W], w_out[M_W:]], xt, gate, seq)


def _odd_mixer(xt, g, sc, sh, gate, w_in, qn_g, kn_g, lam_p, onorm_g, w_out, lam_init, bsz, seq):
    z = norm_linear(xt, g, sc, sh, w_in.astype(BF16), jnp.zeros((w_in.shape[1],), F32), seq)
    nblk = 2 * C_HEADS * C_DH // LANE
    qn, kn = head_norm(z, 0, nblk, nblk, qn_g, kn_g, False)
    o = diff_attention(qn, kn, z, 2 * nblk, lam_p, onorm_g, lam_init, bsz, seq)
    return linear_residual([o], [w_out.astype(BF16)], xt, gate, seq)


def kernel(x, c, ada_w, ada_b, norm_mix_g, norm_ffn_g, ev_w_in, ev_conv_w, ev_igate_b, ev_fgate_b, ev_mnorm_g,
           ev_qn_g, ev_kn_g, ev_w_out, od_w_in, od_qn_g, od_kn_g, od_lam, od_onorm_g, od_w_out, peer_wq,
           peer_keys, peer_u, peer_v):
    bsz, seq, d = x.shape
    depth = ada_w.shape[0]
    mod = adaln_mod(c, ada_w, ada_b).reshape(depth, bsz, 6, 1, d)
    xt = x.reshape(bsz * seq, d)
    for layer in range(depth):
        sh1, sc1, g1, sh2, sc2, g2 = [mod[layer, :, i] for i in range(6)]
        if layer % 2 == 0:
            e = layer // 2
            xt = _even_mixer(xt, norm_mix_g[layer], sc1, sh1, g1, ev_w_in[e], ev_conv_w[e], ev_igate_b[e],
                             ev_fgate_b[e], ev_mnorm_g[e], ev_qn_g[e], ev_kn_g[e], ev_w_out[e], bsz, seq)
        else:
            o = layer // 2
            lam_init = 0.8 - 0.6 * math.exp(-0.3 * layer)
            xt = _odd_mixer(xt, norm_mix_g[layer], sc1, sh1, g1, od_w_in[o], od_qn_g[o], od_kn_g[o], od_lam[o],
                            od_onorm_g[o], od_w_out[o], lam_init, bsz, seq)
        xt = peer_ffn(xt, norm_ffn_g[layer], sc2, sh2, g2, peer_wq[layer].astype(BF16), peer_keys[layer],
                      pack_table(peer_u[layer]), pack_table(peer_v[layer]), seq)
    return xt.reshape(bsz, seq, d)
```

```python
import functools
import math

import jax
import jax.numpy as jnp
from jax import lax
from jax.experimental import pallas as pl
from jax.experimental.pallas import tpu as pltpu

F32 = jnp.float32
BF16 = jnp.bfloat16
HIGHEST = lax.Precision.HIGHEST
NEG_INF = float("-inf")

EPS = 1e-6
D_MODEL = 1024
M_HEADS, M_DH, M_CHUNK, CONV_K = 4, 128, 128, 4
M_W = M_HEADS * M_DH
B_HEADS, B_DH, MOBA_BLOCK, MOBA_TOPK = 8, 64, 256, 3
B_W = B_HEADS * B_DH
C_HEADS, C_DH, C_VDH = 8, 64, 128
PEER_HEADS, PEER_NKEYS, PEER_DK, PEER_TOPK = 8, 128, 256, 16

LANE = 128
VMEM_LIMIT = 56 * 1024 * 1024


def _cparams(sem, vmem=None):
    return pltpu.CompilerParams(dimension_semantics=sem, vmem_limit_bytes=vmem)


def _mod_kernel(c_ref, w_ref, b_ref, o_ref):
    c = c_ref[...]
    ca = c * (1.0 / (1.0 + jnp.exp(-c)))
    o_ref[...] = jnp.dot(ca, w_ref[...], precision=HIGHEST, preferred_element_type=F32) + b_ref[...]


def adaln_mod(c, ada_w, ada_b):
    depth, d, n = ada_w.shape
    bsz = c.shape[0]
    tn = 512
    return pl.pallas_call(
        _mod_kernel,
        grid=(depth, n // tn),
        in_specs=[pl.BlockSpec((bsz, d), lambda l, j: (0, 0)),
                  pl.BlockSpec((None, d, tn), lambda l, j: (l, 0, j)),
                  pl.BlockSpec((None, 1, tn), lambda l, j: (l, 0, j))],
        out_specs=pl.BlockSpec((None, bsz, tn), lambda l, j: (l, 0, j)),
        out_shape=jax.ShapeDtypeStruct((depth, bsz, n), F32),
        compiler_params=_cparams(("parallel", "parallel")),
        name="adaln_mod",
    )(c, ada_w, ada_b.reshape(depth, 1, n))


def _norm_linear_kernel(x_ref, g_ref, sc_ref, sh_ref, w_ref, b_ref, o_ref, *rest, emit_h, precise):
    if emit_h:
        h_out_ref, h_sc = rest
    else:
        (h_sc,) = rest

    @pl.when(pl.program_id(1) == 0)
    def _():
        x = x_ref[...]
        y = x * lax.rsqrt(jnp.mean(x * x, axis=-1, keepdims=True) + EPS) * g_ref[...]
        h = y * (1.0 + sc_ref[...]) + sh_ref[...]
        h_sc[...] = h.astype(h_sc.dtype)
        if emit_h:
            h_out_ref[...] = h

    if precise:
        acc = jnp.dot(h_sc[...], w_ref[...], precision=HIGHEST, preferred_element_type=F32)
    else:
        acc = jnp.dot(h_sc[...], w_ref[...], preferred_element_type=F32)
    o_ref[...] = (acc + b_ref[...]).astype(o_ref.dtype)


def norm_linear(x, g, sc, sh, w, bias, seq, *, tm=1024, tn=512, emit_h=False, out_dtype=F32):
    t, d = x.shape
    n = w.shape[1]
    tn = min(tn, n)
    tm = min(tm, seq)
    precise = w.dtype == F32
    rows_per_batch = seq // tm
    out_shape = [jax.ShapeDtypeStruct((t, n), out_dtype)]
    out_specs = [pl.BlockSpec((tm, tn), lambda i, j: (i, j))]
    if emit_h:
        out_shape.append(jax.ShapeDtypeStruct((t, d), F32))
        out_specs.append(pl.BlockSpec((tm, d), lambda i, j: (i, 0)))
    res = pl.pallas_call(
        functools.partial(_norm_linear_kernel, emit_h=emit_h, precise=precise),
        grid=(t // tm, n // tn),
        in_specs=[pl.BlockSpec((tm, d), lambda i, j: (i, 0)),
                  pl.BlockSpec((1, d), lambda i, j: (0, 0)),
                  pl.BlockSpec((None, 1, d), lambda i, j: (i // rows_per_batch, 0, 0)),
                  pl.BlockSpec((None, 1, d), lambda i, j: (i // rows_per_batch, 0, 0)),
                  pl.BlockSpec((d, tn), lambda i, j: (0, j)),
                  pl.BlockSpec((1, tn), lambda i, j: (0, j))],
        out_specs=out_specs,
        out_shape=out_shape,
        scratch_shapes=[pltpu.VMEM((tm, d), w.dtype)],
        compiler_params=_cparams(("parallel", "arbitrary"), VMEM_LIMIT),
        name="norm_linear",
    )(x, g.reshape(1, d), sc, sh, w, bias.reshape(1, n))
    return res if emit_h else res[0]


def _linear_residual_kernel(*refs, nparts):
    y_refs, w_refs = refs[:nparts], refs[nparts:2 * nparts]
    x_ref, gate_ref, o_ref = refs[2 * nparts:]
    acc = None
    for y_ref, w_ref in zip(y_refs, w_refs):
        part = jnp.dot(y_ref[...].astype(BF16), w_ref[...], preferred_element_type=F32)
        acc = part if acc is None else acc + part
    o_ref[...] = x_ref[...] + gate_ref[...] * acc


def linear_residual(ys, ws, x, gate, seq, *, tm=512):
    t, d = x.shape
    rows_per_batch = seq // tm
    return pl.pallas_call(
        functools.partial(_linear_residual_kernel, nparts=len(ys)),
        grid=(t // tm,),
        in_specs=([pl.BlockSpec((tm, y.shape[1]), lambda i: (i, 0)) for y in ys]
                  + [pl.BlockSpec(w.shape, lambda i: (0, 0)) for w in ws]
                  + [pl.BlockSpec((tm, d), lambda i: (i, 0)),
                     pl.BlockSpec((None, 1, d), lambda i: (i // rows_per_batch, 0, 0))]),
        out_specs=pl.BlockSpec((tm, d), lambda i: (i, 0)),
        out_shape=jax.ShapeDtypeStruct((t, d), F32),
        compiler_params=_cparams(("parallel",), VMEM_LIMIT),
        name="linear_residual",
    )(*ys, *ws, x, gate)


def _shift_rows(x, tail, s):
    if s == 0:
        return x
    xs = pltpu.roll(x, s, axis=0)
    ts = pltpu.roll(tail, s, axis=0)
    row = lax.broadcasted_iota(jnp.int32, (8, x.shape[1]), 0)
    top = jnp.where(row < s, ts, xs[0:8])
    return jnp.concatenate([top, xs[8:]], axis=0)


def _conv_silu(x, tail, w):
    acc = None
    for j in range(CONV_K):
        term = w[j:j + 1, :] * _shift_rows(x, tail, CONV_K - 1 - j)
        acc = term if acc is None else acc + term
    return acc * (1.0 / (1.0 + jnp.exp(-acc)))


def _log_sigmoid(x):
    return jnp.minimum(x, 0.0) - jnp.log(1.0 + jnp.exp(-jnp.abs(x)))


def _mlstm_kernel(q_ref, k_ref, v_ref, o_ref, wq_ref, wk_ref, icol_ref, fcol_ref, irow_ref, frow_ref,
                  g_ref, out_ref, c_sc, n_sc, m_sc, qt_sc, kt_sc):
    L = M_CHUNK

    @pl.when(pl.program_id(2) == 0)
    def _():
        c_sc[...] = jnp.zeros_like(c_sc)
        n_sc[...] = jnp.zeros_like(n_sc)
        m_sc[...] = jnp.zeros_like(m_sc)
        qt_sc[...] = jnp.zeros_like(qt_sc)
        kt_sc[...] = jnp.zeros_like(kt_sc)

    q_raw = q_ref[...]
    k_raw = k_ref[...]
    q = _conv_silu(q_raw, qt_sc[...], wq_ref[...])
    k = _conv_silu(k_raw, kt_sc[...], wk_ref[...]) * (M_DH ** -0.5)
    qt_sc[...] = q_raw[L - 8:, :]
    kt_sc[...] = k_raw[L - 8:, :]
    v = v_ref[...]

    row = lax.broadcasted_iota(jnp.int32, (L, L), 0)
    col = lax.broadcasted_iota(jnp.int32, (L, L), 1)
    causal = row >= col
    tri = jnp.where(causal, 1.0, 0.0).astype(F32)
    triu = jnp.where(row <= col, 1.0, 0.0).astype(F32)

    lf_col = _log_sigmoid(fcol_ref[...])
    lf_row = _log_sigmoid(frow_ref[...])
    ig_col = icol_ref[...]
    ig_row = irow_ref[...]
    b_col = jnp.dot(tri, jnp.broadcast_to(lf_col, (L, L)), precision=HIGHEST,
                    preferred_element_type=F32)[:, 0:1]
    b_row = jnp.dot(jnp.broadcast_to(lf_row, (8, L)), triu, precision=HIGHEST,
                    preferred_element_type=F32)[0:1, :]
    b_end = b_row[:, L - 1:L]

    m_prev = m_sc[...]
    c_prev = c_sc[...]
    n_prev = n_sc[...]

    log_d = jnp.where(causal, b_col - b_row + ig_row, NEG_INF)
    a_inter = b_col + m_prev
    m_t = jnp.maximum(a_inter, jnp.max(log_d, axis=1, keepdims=True))
    w_inter = jnp.exp(a_inter - m_t)
    qb = q.astype(BF16)
    kb = k.astype(BF16)
    vb = v.astype(BF16)
    s = lax.dot_general(qb, kb, (((1,), (1,)), ((), ())), preferred_element_type=F32)
    s = s * jnp.exp(log_d - m_t)
    num = (w_inter * jnp.dot(qb, c_prev.astype(BF16), preferred_element_type=F32)
           + jnp.dot(s.astype(BF16), vb, preferred_element_type=F32))
    den = (w_inter * jnp.sum(q * n_prev, axis=1, keepdims=True) + jnp.sum(s, axis=1, keepdims=True))
    h = num / jnp.maximum(jnp.abs(den), jnp.exp(-m_t))

    w_end_row = b_end - b_row + ig_row
    m_loc = jnp.max(w_end_row, axis=1, keepdims=True)
    e_col = jnp.exp(b_end - b_col + ig_col - m_loc)
    ke = k * e_col
    c_loc = lax.dot_general(ke.astype(BF16), vb, (((0,), (0,)), ((), ())), preferred_element_type=F32)
    n_loc = jnp.sum(ke, axis=0, keepdims=True)
    m_new = jnp.maximum(b_end + m_prev, m_loc)
    a = jnp.exp(b_end + m_prev - m_new)
    sc = jnp.exp(m_loc - m_new)
    c_sc[...] = a * c_prev + sc * c_loc
    n_sc[...] = a * n_prev + sc * n_loc
    m_sc[...] = m_new

    hn = h * lax.rsqrt(jnp.mean(h * h, axis=1, keepdims=True) + EPS) * g_ref[...]
    og = o_ref[...]
    out_ref[...] = (hn * (1.0 / (1.0 + jnp.exp(-og)))).astype(out_ref.dtype)


def mlstm_mixer(z, conv_w, icol, fcol, irow, frow, mnorm_g, bsz, seq):
    t = bsz * seq
    L = M_CHUNK
    nc = seq // L
    hq = M_W // LANE

    def zspec(sec):
        return pl.BlockSpec((L, M_DH), lambda b, h, c, sec=sec: (b * nc + c, sec * hq + h))

    return pl.pallas_call(
        _mlstm_kernel,
        grid=(bsz, M_HEADS, nc),
        in_specs=[zspec(0), zspec(1), zspec(2), zspec(3),
                  pl.BlockSpec((CONV_K, M_DH), lambda b, h, c: (0, h)),
                  pl.BlockSpec((CONV_K, M_DH), lambda b, h, c: (0, hq + h)),
                  pl.BlockSpec((None, L, 1), lambda b, h, c: (h, b * nc + c, 0)),
                  pl.BlockSpec((None, L, 1), lambda b, h, c: (h, b * nc + c, 0)),
                  pl.BlockSpec((None, None, 1, L), lambda b, h, c: (h, b * nc + c, 0, 0)),
                  pl.BlockSpec((None, None, 1, L), lambda b, h, c: (h, b * nc + c, 0, 0)),
                  pl.BlockSpec((None, 1, M_DH), lambda b, h, c: (h, 0, 0))],
        out_specs=pl.BlockSpec((L, M_DH), lambda b, h, c: (b * nc + c, h)),
        out_shape=jax.ShapeDtypeStruct((t, M_W), F32),
        scratch_shapes=[pltpu.VMEM((M_DH, M_DH), F32), pltpu.VMEM((1, M_DH), F32), pltpu.VMEM((1, 1), F32),
                        pltpu.VMEM((8, M_DH), F32), pltpu.VMEM((8, M_DH), F32)],
        compiler_params=_cparams(("parallel", "parallel", "arbitrary")),
        name="mlstm",
    )(z, z, z, z, conv_w, conv_w, icol, fcol, irow, frow, mnorm_g.reshape(M_HEADS, 1, M_DH))


def split_gates(zg):
    t = zg.shape[0]
    gt = zg[:, :2 * M_HEADS].T
    cols = gt.reshape(2 * M_HEADS, t, 1)
    rows = gt.reshape(2 * M_HEADS, t // M_CHUNK, 1, M_CHUNK)
    return cols[:M_HEADS], cols[M_HEADS:], rows[:M_HEADS], rows[M_HEADS:]


ATT_BLOCK = 256
ATT_UNROLL = 4
HALF = 64


def _half_rms_norm(x, g):
    lo = lax.broadcasted_iota(jnp.int32, x.shape, 1) < HALF
    x2 = x * x
    s_lo = jnp.sum(jnp.where(lo, x2, 0.0), axis=1, keepdims=True)
    s_hi = jnp.sum(jnp.where(lo, 0.0, x2), axis=1, keepdims=True)
    r = jnp.where(lo, lax.rsqrt(s_lo * (1.0 / HALF) + EPS), lax.rsqrt(s_hi * (1.0 / HALF) + EPS))
    return x * r * g


def _head_norm_kernel(q_ref, k_ref, gq_ref, gk_ref, qn_ref, kn_ref, *rest):
    for c in range(q_ref.shape[1] // LANE):
        cols = slice(c * LANE, (c + 1) * LANE)
        qn_ref[:, cols] = _half_rms_norm(q_ref[:, cols], gq_ref[...])
        kn = _half_rms_norm(k_ref[:, cols], gk_ref[...])
        kn_ref[:, cols] = kn.astype(kn_ref.dtype)
        if rest:
            rest[0][:, cols] = jnp.mean(kn, axis=0, keepdims=True)


def head_norm(z, q_blk0, k_blk0, nblk, gq, gk, with_kmean):
    t = z.shape[0]
    w = nblk * LANE
    tb = ATT_BLOCK
    gq2 = jnp.concatenate([gq, gq]).reshape(1, LANE)
    gk2 = jnp.concatenate([gk, gk]).reshape(1, LANE)
    assert q_blk0 % nblk == 0 and k_blk0 % nblk == 0
    out_shape = [jax.ShapeDtypeStruct((t, w), F32), jax.ShapeDtypeStruct((t, w), BF16)]
    out_specs = [pl.BlockSpec((tb, w), lambda i: (i, 0)), pl.BlockSpec((tb, w), lambda i: (i, 0))]
    if with_kmean:
        out_shape.append(jax.ShapeDtypeStruct((t // tb, 1, w), F32))
        out_specs.append(pl.BlockSpec((None, 1, w), lambda i: (i, 0, 0)))
    return pl.pallas_call(
        _head_norm_kernel,
        grid=(t // tb,),
        in_specs=[pl.BlockSpec((tb, w), lambda i: (i, q_blk0 // nblk)),
                  pl.BlockSpec((tb, w), lambda i: (i, k_blk0 // nblk)),
                  pl.BlockSpec((1, LANE), lambda i: (0, 0)),
                  pl.BlockSpec((1, LANE), lambda i: (0, 0))],
        out_specs=out_specs,
        out_shape=out_shape,
        compiler_params=_cparams(("parallel",)),
        name="head_norm",
    )(z, z, gq2, gk2)


def _attn_kernel(*refs, moba, lam_init):
    if moba:
        (slope_ref, q_ref, k_ref, vt_ref, km_ref, out_ref, sel_sc, m_sc, l_sc, acc_sc) = refs
    else:
        (slope_ref, q_ref, k_ref, vt_ref, lam_ref, g_ref, out_ref, m_sc, l_sc, acc_sc) = refs
    tb = ATT_BLOCK
    grp = pl.program_id(1)
    j = pl.program_id(2)
    scale = HALF ** -0.5

    q = q_ref[...]
    lo = lax.broadcasted_iota(jnp.int32, (tb, LANE), 1) < HALF
    qs = q * scale
    q_half = [jnp.where(lo, qs, 0.0).astype(BF16), jnp.where(lo, 0.0, qs).astype(BF16)]
    krow = lax.broadcasted_iota(jnp.int32, (tb, tb), 0)
    causal = krow <= lax.broadcasted_iota(jnp.int32, (tb, tb), 1)
    if moba:
        slopes = [slope_ref[2 * grp], slope_ref[2 * grp + 1]]
    else:
        slopes = [slope_ref[grp], slope_ref[grp]]
    kpos = krow.astype(F32)
    bias = [slopes[0] * kpos, slopes[1] * kpos] if moba else [slopes[0] * kpos] * 2

    if moba:
        km = km_ref[...]
        blk = lax.broadcasted_iota(jnp.int32, (LANE, tb), 0)
        for hf in range(2):
            qm = jnp.where(lo, q, 0.0) if hf == 0 else jnp.where(lo, 0.0, q)
            gate = lax.dot_general(km, qm, (((1,), (1,)), ((), ())), precision=HIGHEST,
                                   preferred_element_type=F32)
            gate = jnp.where(blk < j, gate, NEG_INF)
            sel = jnp.zeros((LANE, tb), F32)
            for _ in range(MOBA_TOPK):
                mx = jnp.max(gate, axis=0, keepdims=True)
                idx = jnp.min(jnp.where(gate == mx, blk, LANE), axis=0, keepdims=True)
                hit = blk == idx
                sel = jnp.where(jnp.logical_and(hit, mx > NEG_INF), 1.0, sel)
                gate = jnp.where(hit, NEG_INF, gate)
            sel_sc[hf] = sel

    m_sc[...] = jnp.full(m_sc.shape, NEG_INF, F32)
    l_sc[...] = jnp.zeros(l_sc.shape, F32)
    acc_sc[...] = jnp.zeros(acc_sc.shape, F32)

    def block_update(blocks, diag):
        m_old = [m_sc[0], m_sc[1]]
        l_old = [l_sc[0], l_sc[1]]
        acc_old = [acc_sc[0], acc_sc[1]]
        xs, shifts, vts = [], [], []
        for n in blocks:
            start = pl.multiple_of(n * tb, tb)
            kb = k_ref[pl.ds(start, tb), :]
            vts.append(vt_ref[n])
            off = ((n - j) * tb).astype(F32)
            for hf in range(2):
                x = lax.dot_general(kb, q_half[hf], (((1,), (1,)), ((), ())), preferred_element_type=F32)
                x = x + bias[hf]
                if diag:
                    x = jnp.where(causal, x, NEG_INF)
                elif moba:
                    x = jnp.where(sel_sc[hf, pl.ds(n, 1), :] > 0.5, x, NEG_INF)
                xs.append(x)
                shifts.append(slopes[hf] * off)
        for hf in range(2):
            ids = [2 * b + hf for b in range(len(blocks))]
            m_new = m_old[hf]
            for i in ids:
                m_new = jnp.maximum(m_new, jnp.max(xs[i], axis=0, keepdims=True) + shifts[i])
            alpha = jnp.exp(m_old[hf] - m_new)
            l_new = alpha * l_old[hf]
            acc_new = alpha * acc_old[hf]
            for b, i in enumerate(ids):
                p = jnp.exp(xs[i] - (m_new - shifts[i]))
                l_new = l_new + jnp.sum(p, axis=0, keepdims=True)
                acc_new = acc_new + jnp.dot(vts[b], p.astype(BF16), preferred_element_type=F32)
            l_sc[hf] = l_new
            acc_sc[hf] = acc_new
            m_sc[hf] = m_new

    block_update([j], True)

    def body(i, carry):
        block_update([ATT_UNROLL * i + u for u in range(ATT_UNROLL)], False)
        return carry

    lax.fori_loop(0, j // ATT_UNROLL, body, 0)

    done = (j // ATT_UNROLL) * ATT_UNROLL
    width = ATT_UNROLL // 2
    while width >= 1:
        has = (j & width) != 0

        @pl.when(has)
        def _(done=done, width=width):
            block_update([done + u for u in range(width)], False)

        done = done + jnp.where(has, width, 0)
        width //= 2

    o0 = acc_sc[0] * (1.0 / l_sc[0])
    o1 = acc_sc[1] * (1.0 / l_sc[1])
    if moba:
        first_head = lax.broadcasted_iota(jnp.int32, (LANE, tb), 0) < HALF
        out_ref[...] = jnp.where(first_head, o0, o1).T.astype(out_ref.dtype)
    else:
        lp = lam_ref[...]
        s1 = jnp.sum(lp[0:1] * lp[1:2], axis=1, keepdims=True)
        s2 = jnp.sum(lp[2:3] * lp[3:4], axis=1, keepdims=True)
        lam = jnp.exp(s1) - jnp.exp(s2) + lam_init
        o = (o0 - lam * o1).T
        o = o * lax.rsqrt(jnp.mean(o * o, axis=1, keepdims=True) + EPS) * g_ref[...]
        out_ref[...] = (o * (1.0 - lam_init)).astype(out_ref.dtype)


def _alibi_slopes(n_heads):
    return jnp.asarray([2.0 ** (-8.0 * (h + 1) / n_heads) for h in range(n_heads)], dtype=F32)


def _value_blocks_t(z, v_blk0, ngrp, bsz, seq):
    nb = seq // ATT_BLOCK
    v = z[:, v_blk0 * LANE:(v_blk0 + ngrp) * LANE].astype(BF16)
    return v.reshape(bsz, nb, ATT_BLOCK, ngrp, LANE).transpose(0, 3, 1, 4, 2)


def _attn_scratch(tb):
    return [pltpu.VMEM((2, 1, tb), F32), pltpu.VMEM((2, 1, tb), F32), pltpu.VMEM((2, LANE, tb), F32)]


def moba_attention(qn, kn, kmean, z, v_blk0, bsz, seq):
    t = bsz * seq
    tb = ATT_BLOCK
    nb = seq // tb
    ngrp = B_W // LANE
    km = jnp.pad(kmean.reshape(bsz, nb, B_W), ((0, 0), (0, LANE - nb), (0, 0)))
    return pl.pallas_call(
        functools.partial(_attn_kernel, moba=True, lam_init=None),
        grid=(bsz, ngrp, nb),
        in_specs=[pl.BlockSpec(memory_space=pltpu.SMEM),
                  pl.BlockSpec((tb, LANE), lambda b, g, j: (b * nb + j, g)),
                  pl.BlockSpec((seq, LANE), lambda b, g, j: (b, g)),
                  pl.BlockSpec((None, None, nb, LANE, tb), lambda b, g, j: (b, g, 0, 0, 0)),
                  pl.BlockSpec((None, LANE, LANE), lambda b, g, j: (b, 0, g))],
        out_specs=pl.BlockSpec((tb, LANE), lambda b, g, j: (b * nb + j, g)),
        out_shape=jax.ShapeDtypeStruct((t, B_W), F32),
        scratch_shapes=[pltpu.VMEM((2, LANE, tb), F32)] + _attn_scratch(tb),
        compiler_params=_cparams(("parallel", "parallel", "arbitrary"), VMEM_LIMIT),
        name="moba_attention",
    )(_alibi_slopes(B_HEADS), qn, kn, _value_blocks_t(z, v_blk0, ngrp, bsz, seq), km)


def diff_attention(qn, kn, z, v_blk0, lam_p, onorm_g, lam_init, bsz, seq):
    t = bsz * seq
    tb = ATT_BLOCK
    nb = seq // tb
    return pl.pallas_call(
        functools.partial(_attn_kernel, moba=False, lam_init=lam_init),
        grid=(bsz, C_HEADS, nb),
        in_specs=[pl.BlockSpec(memory_space=pltpu.SMEM),
                  pl.BlockSpec((tb, LANE), lambda b, g, j: (b * nb + j, g)),
                  pl.BlockSpec((seq, LANE), lambda b, g, j: (b, g)),
                  pl.BlockSpec((None, None, nb, LANE, tb), lambda b, g, j: (b, g, 0, 0, 0)),
                  pl.BlockSpec((4, C_DH), lambda b, g, j: (0, 0)),
                  pl.BlockSpec((1, C_VDH), lambda b, g, j: (0, 0))],
        out_specs=pl.BlockSpec((tb, LANE), lambda b, g, j: (b * nb + j, g)),
        out_shape=jax.ShapeDtypeStruct((t, C_HEADS * C_VDH), F32),
        scratch_shapes=_attn_scratch(tb),
        compiler_params=_cparams(("parallel", "parallel", "arbitrary"), VMEM_LIMIT),
        name="diff_attention",
    )(_alibi_slopes(C_HEADS), qn, kn, _value_blocks_t(z, v_blk0, C_HEADS, bsz, seq), lam_p,
      onorm_g.reshape(1, C_VDH))


def _top_rows(s, count):
    n = s.shape[0]
    rows = lax.broadcasted_iota(jnp.int32, s.shape, 0)
    vals, idxs = [], []
    for _ in range(count):
        mx = jnp.max(s, axis=0, keepdims=True)
        ix = jnp.min(jnp.where(s == mx, rows, n), axis=0, keepdims=True)
        vals.append(mx)
        idxs.append(ix)
        s = jnp.where(rows == ix, NEG_INF, s)
    return jnp.concatenate(vals, axis=0), jnp.concatenate(idxs, axis=0)


def _peer_route_kernel(q_ref, keys_ref, eid_ref, gate_ref):
    kk = PEER_TOPK
    half = PEER_DK // 2
    tops = []
    for p in range(2):
        qp = q_ref[:, p * half:(p + 1) * half]
        st = lax.dot_general(keys_ref[p], qp, (((1,), (1,)), ((), ())), precision=HIGHEST,
                             preferred_element_type=F32)
        tops.append(_top_rows(st, kk))
    (s0, i0), (s1, i1) = tops
    widths = [kk // (a + 1) for a in range(kk)]
    pad = -sum(widths) % 8
    cand_s = jnp.concatenate([s0[a:a + 1] + s1[0:widths[a]] for a in range(kk)]
                             + [jnp.full((pad, s0.shape[1]), NEG_INF, F32)], axis=0)
    cand_i = jnp.concatenate([i0[a:a + 1] * PEER_NKEYS + i1[0:widths[a]] for a in range(kk)]
                             + [jnp.zeros((pad, s0.shape[1]), jnp.int32)], axis=0)
    score, pos = _top_rows(cand_s, kk)
    rows = lax.broadcasted_iota(jnp.int32, cand_i.shape, 0)
    eid = jnp.concatenate(
        [jnp.sum(jnp.where(rows == pos[a:a + 1], cand_i, 0), axis=0, keepdims=True) for a in range(kk)], axis=0)
    e = jnp.exp(score - score[0:1])
    eid_ref[...] = eid
    gate_ref[...] = e / jnp.sum(e, axis=0, keepdims=True)


def peer_route(qry, keys, *, tt=512):
    t = qry.shape[0]
    hk = PEER_HEADS * PEER_TOPK
    return pl.pallas_call(
        _peer_route_kernel,
        grid=(t // tt, PEER_HEADS),
        in_specs=[pl.BlockSpec((tt, PEER_DK), lambda i, h: (i, h)),
                  pl.BlockSpec((2, PEER_NKEYS, PEER_DK // 2), lambda i, h: (0, 0, 0))],
        out_specs=[pl.BlockSpec((PEER_TOPK, tt), lambda i, h: (h, i)),
                   pl.BlockSpec((PEER_TOPK, tt), lambda i, h: (h, i))],
        out_shape=[jax.ShapeDtypeStruct((hk, t), jnp.int32), jax.ShapeDtypeStruct((hk, t), F32)],
        compiler_params=_cparams(("parallel", "parallel")),
        name="peer_route",
    )(qry, keys)


ROW_SUB = 4
PEER_TT = 128


PEER_HK = PEER_HEADS * PEER_TOPK
ROW_BF = 2 * ROW_SUB
GROUP = 8
NSLOT = 2


def pack_table(tab):
    n, d = tab.shape
    b = lax.bitcast_convert_type(tab.astype(BF16), jnp.uint16).astype(jnp.uint32)
    b = b.reshape(n, ROW_SUB, 2, LANE)
    return b[:, :, 0, :] | (b[:, :, 1, :] << 16)


def _split_bf16(x):
    hi = x.astype(BF16)
    lo = (x - hi.astype(F32)).astype(BF16)
    return jnp.concatenate([hi, lo], axis=0)


def _gelu(x):
    return 0.5 * x * (1.0 + lax.erf(x * (2.0 ** -0.5)))


def _gather_rows(idx_refs, tab_ref, dst_ref, t):
    for k in range(PEER_HK):
        dst_ref[k * ROW_SUB:(k + 1) * ROW_SUB, :] = tab_ref[idx_refs[k // PEER_TOPK][k % PEER_TOPK, t]]


def _next_token(t0, i):
    assert GROUP % NSLOT == 0
    if i + 1 < GROUP:
        return t0 + i + 1
    return jnp.minimum(t0 + GROUP, PEER_TT - 1)


def _diag_mask(rows):
    shape = (rows, PEER_HK * ROW_BF)
    return (lax.broadcasted_iota(jnp.int32, shape, 1) % ROW_BF) == (lax.broadcasted_iota(jnp.int32, shape, 0) % ROW_BF)


def _peer_up_kernel(*refs):
    idx_refs = refs[:PEER_HEADS]
    h_ref, gate_ref, tab_ref, act_ref = refs[PEER_HEADS:PEER_HEADS + 4]
    slots = refs[PEER_HEADS + 4:PEER_HEADS + 4 + NSLOT]
    c_sc = refs[PEER_HEADS + 4 + NSLOT]
    wide = PEER_HK * ROW_BF
    mask = _diag_mask(2 * ROW_BF)
    fold = (lax.broadcasted_iota(jnp.int32, (wide, PEER_HK), 0) // ROW_BF
            == lax.broadcasted_iota(jnp.int32, (wide, PEER_HK), 1)).astype(BF16)

    def group(g, carry):
        t0 = g * GROUP
        for i in range(GROUP):
            _gather_rows(idx_refs, tab_ref, slots[(i + 1) % NSLOT], _next_token(t0, i))
            rows = pltpu.bitcast(slots[i % NSLOT][...], BF16)
            xs = _split_bf16(h_ref[t0 + i])
            y = lax.dot_general(xs, rows, (((1,), (1,)), ((), ())), preferred_element_type=F32)
            y = jnp.where(mask, y, 0.0)
            c_sc[i:i + 1, :] = jnp.sum(y, axis=0, keepdims=True)
        s = jnp.dot(_split_bf16(c_sc[...]), fold, preferred_element_type=F32)
        s = s[0:GROUP] + s[GROUP:]
        rows8 = pl.ds(pl.multiple_of(t0, GROUP), GROUP)
        act_ref[rows8, :] = _gelu(s) * gate_ref[rows8, :]
        return carry

    _gather_rows(idx_refs, tab_ref, slots[0], 0)
    lax.fori_loop(0, PEER_TT // GROUP, group, 0)


def _peer_down_kernel(*refs):
    idx_refs = refs[:PEER_HEADS]
    act_ref, x_ref, g_ref, tab_ref, out_ref = refs[PEER_HEADS:PEER_HEADS + 5]
    slots = refs[PEER_HEADS + 5:PEER_HEADS + 5 + NSLOT]
    wide = PEER_HK * ROW_BF
    mask = _diag_mask(ROW_BF)
    spread = (lax.broadcasted_iota(jnp.int32, (PEER_HK, wide), 1) // ROW_BF
              == lax.broadcasted_iota(jnp.int32, (PEER_HK, wide), 0)).astype(BF16)
    gate = g_ref[...]

    def group(g, carry):
        t0 = g * GROUP
        acts = act_ref[pl.ds(pl.multiple_of(t0, GROUP), GROUP), :]
        a_wide = jnp.dot(_split_bf16(acts), spread, preferred_element_type=F32)
        for i in range(GROUP):
            _gather_rows(idx_refs, tab_ref, slots[(i + 1) % NSLOT], _next_token(t0, i))
            rows = pltpu.bitcast(slots[i % NSLOT][...], BF16)
            a_hi = jnp.where(mask, jnp.broadcast_to(a_wide[i:i + 1], (ROW_BF, wide)), 0.0)
            a_lo = jnp.where(mask, jnp.broadcast_to(a_wide[GROUP + i:GROUP + i + 1], (ROW_BF, wide)), 0.0)
            lhs = jnp.concatenate([a_hi, a_lo], axis=0).astype(BF16)
            y = jnp.dot(lhs, rows, preferred_element_type=F32)
            out_ref[t0 + i] = x_ref[t0 + i] + gate * (y[0:ROW_BF] + y[ROW_BF:])
        return carry

    _gather_rows(idx_refs, tab_ref, slots[0], 0)
    lax.fori_loop(0, PEER_TT // GROUP, group, 0)


def _table_spec(n):
    return pl.BlockSpec((n, ROW_SUB, LANE), lambda i: (0, 0, 0), pipeline_mode=pl.Buffered(1))


def _idx_specs(tt):
    return [pl.BlockSpec((PEER_TOPK, tt), lambda i: (0, i), memory_space=pltpu.SMEM) for _ in range(PEER_HEADS)]


def peer_up(idxs, h, gate, tab_u):
    t, hk = gate.shape
    n = tab_u.shape[0]
    tt = PEER_TT
    return pl.pallas_call(
        _peer_up_kernel,
        grid=(t // tt,),
        in_specs=_idx_specs(tt) + [
                  pl.BlockSpec((tt, 8, LANE), lambda i: (i, 0, 0)),
                  pl.BlockSpec((tt, hk), lambda i: (i, 0)),
                  _table_spec(n)],
        out_specs=pl.BlockSpec((tt, hk), lambda i: (i, 0)),
        out_shape=jax.ShapeDtypeStruct((t, hk), F32),
        scratch_shapes=[pltpu.VMEM((hk * ROW_SUB, LANE), jnp.uint32) for _ in range(NSLOT)]
                       + [pltpu.VMEM((GROUP, hk * ROW_BF), F32)],
        compiler_params=_cparams(("arbitrary",), VMEM_LIMIT),
        name="peer_up",
    )(*idxs, h.reshape(t, 8, LANE), gate, tab_u)


def peer_down(idxs, act, x, gate2, tab_v, seq):
    t, hk = act.shape
    n = tab_v.shape[0]
    d = x.shape[1]
    tt = PEER_TT
    tiles_per_batch = seq // tt
    out = pl.pallas_call(
        _peer_down_kernel,
        grid=(t // tt,),
        in_specs=_idx_specs(tt) + [
                  pl.BlockSpec((tt, hk), lambda i: (i, 0)),
                  pl.BlockSpec((tt, 8, LANE), lambda i: (i, 0, 0)),
                  pl.BlockSpec((None, 8, LANE), lambda i: (i // tiles_per_batch, 0, 0)),
                  _table_spec(n)],
        out_specs=pl.BlockSpec((tt, 8, LANE), lambda i: (i, 0, 0)),
        out_shape=jax.ShapeDtypeStruct((t, 8, LANE), F32),
        scratch_shapes=[pltpu.VMEM((hk * ROW_SUB, LANE), jnp.uint32) for _ in range(NSLOT)],
        compiler_params=_cparams(("arbitrary",), VMEM_LIMIT),
        name="peer_down",
    )(*idxs, act, x.reshape(t, 8, LANE), gate2.reshape(-1, 8, LANE), tab_v)
    return out.reshape(t, d)


def peer_ffn(x, g, sc, sh, gate2, wq, keys, tab_u, tab_v, seq):
    qry, h = norm_linear(x, g, sc, sh, wq, jnp.zeros((wq.shape[1],), F32), seq, emit_h=True)
    eid_t, gate_t = peer_route(qry, keys)
    idxs = [eid_t[hd * PEER_TOPK:(hd + 1) * PEER_TOPK] for hd in range(PEER_HEADS)]
    act = peer_up(idxs, h, gate_t.T, tab_u)
    return peer_down(idxs, act, x, gate2, tab_v, seq)


def _even_mixer(xt, g, sc, sh, gate, w_in, conv_w, igate_b, fgate_b, mnorm_g, qn_g, kn_g, w_out, bsz, seq):
    d = xt.shape[1]
    n_gate = 2 * M_HEADS
    g0 = 4 * M_W
    w_main = jnp.concatenate([w_in[:, :g0], w_in[:, g0 + n_gate:]], axis=1).astype(BF16)
    w_gate = jnp.pad(w_in[:, g0:g0 + n_gate], ((0, 0), (0, LANE - n_gate)))
    b_gate = jnp.pad(jnp.concatenate([igate_b, fgate_b]), (0, LANE - n_gate))
    z = norm_linear(xt, g, sc, sh, w_main, jnp.zeros((w_main.shape[1],), F32), seq)
    zg = norm_linear(xt, g, sc, sh, w_gate, b_gate, seq)
    icol, fcol, irow, frow = split_gates(zg)
    hm = mlstm_mixer(z, conv_w, icol, fcol, irow, frow, mnorm_g, bsz, seq)
    blk = g0 // LANE
    nblk = B_W // LANE
    qn, kn, kmean = head_norm(z, blk, blk + nblk, nblk, qn_g, kn_g, True)
    hb = moba_attention(qn, kn, kmean, z, blk + 2 * nblk, bsz, seq)
    w_out = w_out.astype(BF16)
    return linear_residual([hm, hb], [w_out[:M_W], w_out[M_W:]], xt, gate, seq)


def _odd_mixer(xt, g, sc, sh, gate, w_in, qn_g, kn_g, lam_p, onorm_g, w_out, lam_init, bsz, seq):
    z = norm_linear(xt, g, sc, sh, w_in.astype(BF16), jnp.zeros((w_in.shape[1],), F32), seq)
    nblk = 2 * C_HEADS * C_DH // LANE
    qn, kn = head_norm(z, 0, nblk, nblk, qn_g, kn_g, False)
    o = diff_attention(qn, kn, z, 2 * nblk, lam_p, onorm_g, lam_init, bsz, seq)
    return linear_residual([o], [w_out.astype(BF16)], xt, gate, seq)


def kernel(x, c, ada_w, ada_b, norm_mix_g, norm_ffn_g, ev_w_in, ev_conv_w, ev_igate_b, ev_fgate_b, ev_mnorm_g,
           ev_qn_g, ev_kn_g, ev_w_out, od_w_in, od_qn_g, od_kn_g, od_lam, od_onorm_g, od_w_out, peer_wq,
           peer_keys, peer_u, peer_v):
    bsz, seq, d = x.shape
    depth = ada_w.shape[0]
    mod = adaln_mod(c, ada_w, ada_b).reshape(depth, bsz, 6, 1, d)
    xt = x.reshape(bsz * seq, d)
    for layer in range(depth):
        sh1, sc1, g1, sh2, sc2, g2 = [mod[layer, :, i] for i in range(6)]
        if layer % 2 == 0:
            e = layer // 2
            xt = _even_mixer(xt, norm_mix_g[layer], sc1, sh1, g1, ev_w_in[e], ev_conv_w[e], ev_igate_b[e],
                             ev_fgate_b[e], ev_mnorm_g[e], ev_qn_g[e], ev_kn_g[e], ev_w_out[e], bsz, seq)
        else:
            o = layer // 2
            lam_init = 0.8 - 0.6 * math.exp(-0.3 * layer)
            xt = _odd_mixer(xt, norm_mix_g[layer], sc1, sh1, g1, od_w_in[o], od_qn_g[o], od_kn_g[o], od_lam[o],
                            od_onorm_g[o], od_w_out[o], lam_init, bsz, seq)
        xt = peer_ffn(xt, norm_ffn_g[layer], sc2, sh2, g2, peer_wq[layer].astype(BF16), peer_keys[layer],
                      pack_table(peer_u[layer]), pack_table(peer_v[layer]), seq)
    return xt.reshape(bsz, seq, d)
```
